```python
import math
import jax
import jax.numpy as jnp
from jax import lax
import numpy as np

D_MODEL = 2048
BATCH = 8
SEQ = 2048
DEPTH = 2

RET_HEADS = 4
RET_QK_DIM = 256
RET_V_DIM = D_MODEL // RET_HEADS
RET_QK_WIDTH = RET_HEADS * RET_QK_DIM
CHUNK = 128
ROPE_BASE = 10000.0
SSM_GROUP = 16
SSM_GROUPS = D_MODEL // SSM_GROUP
SSM_STATE = 64
DT_MIN = 0.001
DT_MAX = 0.1
D_FF = ((8 * D_MODEL // 3 + 255) // 256) * 256
IN_WIDTH = 2 * RET_QK_WIDTH + 5 * D_MODEL
EPS = 1e-6

kernel_name = "hybrid_retention_s5_gated_encoder"


def rms_norm(x, g):
    xf = x.astype(jnp.float32)
    y = xf * lax.rsqrt(jnp.mean(xf * xf, axis=-1, keepdims=True) + EPS)
    return (y * g.astype(jnp.float32)).astype(x.dtype)


def rotary(x):
    L = x.shape[1]
    half = x.shape[-1] // 2
    inv = 1.0 / (ROPE_BASE ** (jnp.arange(half, dtype=jnp.float32) / half))
    ang = jnp.arange(L, dtype=jnp.float32)[:, None] * inv[None, :]
    cos = jnp.cos(ang)[None, :, None, :]
    sin = jnp.sin(ang)[None, :, None, :]
    xf = x.astype(jnp.float32)
    x1, x2 = xf[..., :half], xf[..., half:]
    return jnp.concatenate([x1 * cos - x2 * sin, x1 * sin + x2 * cos], axis=-1)


def retention(q, k, v, log_gamma):
    f32 = jnp.float32
    b, l, h, dk = q.shape
    dv = v.shape[-1]
    nc = l // CHUNK
    q = q.reshape(b, nc, CHUNK, h, dk)
    k = (k * dk ** -0.5).reshape(b, nc, CHUNK, h, dk)
    v = v.astype(f32).reshape(b, nc, CHUNK, h, dv)
    lg = log_gamma.astype(f32)
    lg_f, lg_b = lg[0], lg[1]
    t = jnp.arange(CHUNK, dtype=f32)
    diff = t[:, None] - t[None, :]
    dmat = jnp.exp(jnp.where(diff >= 0, lg_f[:, None, None] * diff, -lg_b[:, None, None] * diff))
    scores = jnp.einsum('bnthd,bnshd->bnhts', q, k) * dmat
    y = jnp.einsum('bnhts,bnshe->bnthe', scores, v)
    kf = k * jnp.exp(lg_f[None, :] * (CHUNK - 1.0 - t)[:, None])[:, :, None]
    kb = k * jnp.exp(lg_b[None, :] * t[:, None])[:, :, None]
    kv_f = jnp.einsum('bnshd,bnshe->nbhde', kf, v)
    kv_b = jnp.einsum('bnshd,bnshe->nbhde', kb, v)
    decay_f = jnp.exp(lg_f * CHUNK)[None, :, None, None]
    decay_b = jnp.exp(lg_b * CHUNK)[None, :, None, None]

    def step_f(s, kv):
        return decay_f * s + kv, s

    def step_b(s, kv):
        return decay_b * s + kv, s

    zero = jnp.zeros((b, h, dk, dv), f32)
    _, s_f = lax.scan(step_f, zero, kv_f)
    _, s_b = lax.scan(step_b, zero, kv_b, reverse=True)
    qf = q * jnp.exp(lg_f[None, :] * (t[:, None] + 1.0))[:, :, None]
    qb = q * jnp.exp(lg_b[None, :] * (CHUNK - t)[:, None])[:, :, None]
    y = y + jnp.einsum('bnthd,nbhde->bnthe', qf, s_f) + jnp.einsum('bnthd,nbhde->bnthe', qb, s_b)
    return y.reshape(b, l, h, dv)


def _linear_recurrence(e1, e2):
    a1, b1 = e1
    a2, b2 = e2
    return a1 * a2, a2 * b1 + b2


def s5_direction(u, a_re, a_im, log_dt, b_re, b_im, c_re, c_im, reverse):
    f32 = jnp.float32
    lam = lax.complex(a_re.astype(f32), a_im.astype(f32))
    dt = jnp.exp(log_dt.astype(f32))[:, None]
    lam_bar = jnp.exp(lam * dt)
    b_c = lax.complex(b_re.astype(f32), b_im.astype(f32))
    b_bar = ((lam_bar - 1.0) / lam)[:, :, None] * b_c
    bu = lax.complex(jnp.einsum('blgh,gph->blgp', u, jnp.real(b_bar)),
                     jnp.einsum('blgh,gph->blgp', u, jnp.imag(b_bar)))
    a = jnp.broadcast_to(lam_bar, bu.shape)
    _, xs = lax.associative_scan(_linear_recurrence, (a, bu), axis=1, reverse=reverse)
    return (jnp.einsum('blgp,ghp->blgh', jnp.real(xs), c_re.astype(f32))
            - jnp.einsum('blgp,ghp->blgh', jnp.imag(xs), c_im.astype(f32)))


def hybrid_mixer(h, w_in, log_gamma, a_re, a_im, log_dt, b_re, b_im, c_re, c_im,
                 d_skip, w_glu, b_glu, w_out):
    bsz, l, _ = h.shape
    dt_in = h.dtype
    proj = h @ w_in
    cuts = [RET_QK_WIDTH, 2 * RET_QK_WIDTH, 2 * RET_QK_WIDTH + D_MODEL,
            2 * RET_QK_WIDTH + 2 * D_MODEL, 2 * RET_QK_WIDTH + 3 * D_MODEL,
            2 * RET_QK_WIDTH + 4 * D_MODEL]
    q, k, v, g, u, gate_r, gate_s = jnp.split(proj, cuts, axis=-1)

    q = rotary(q.reshape(bsz, l, RET_HEADS, RET_QK_DIM))
    k = rotary(k.reshape(bsz, l, RET_HEADS, RET_QK_DIM))
    v = v.reshape(bsz, l, RET_HEADS, RET_V_DIM)
    y = retention(q, k, v, log_gamma)
    y = y * lax.rsqrt(jnp.mean(y * y, axis=-1, keepdims=True) + EPS)
    ret_out = jax.nn.silu(g.astype(jnp.float32)) * y.reshape(bsz, l, D_MODEL)

    uf = u.astype(jnp.float32)
    ug = uf.reshape(bsz, l, SSM_GROUPS, SSM_GROUP)
    ys = (s5_direction(ug, a_re[0], a_im[0], log_dt[0], b_re[0], b_im[0], c_re[0], c_im[0], False)
          + s5_direction(ug, a_re[1], a_im[1], log_dt[1], b_re[1], b_im[1], c_re[1], c_im[1], True))
    ys = ys.reshape(bsz, l, D_MODEL) + d_skip.astype(jnp.float32) * uf
    ys = jax.nn.gelu(ys).astype(dt_in)
    ssm_out = ys * jax.nn.sigmoid(ys @ w_glu + b_glu)

    merged = (jax.nn.sigmoid(gate_r) * ret_out.astype(dt_in)
              + jax.nn.sigmoid(gate_s) * ssm_out)
    return merged @ w_out


def swiglu(h, w_gate, w_up, w_down):
    return (jax.nn.silu(h @ w_gate) * (h @ w_up)) @ w_down


def setup_inputs(seed: int = 0) -> dict:
    key = jax.random.key(seed)
    ks = jax.random.split(key, 24)
    f32 = jnp.float32
    G, P, Hg = SSM_GROUPS, SSM_STATE, SSM_GROUP
    nrm = lambda k, shape, scale: jax.random.normal(k, shape, f32) * scale
    x = nrm(ks[0], (BATCH, SEQ, D_MODEL), 1.0)
    ln_mix_g = 1.0 + nrm(ks[1], (DEPTH, D_MODEL), 0.02)
    w_in = nrm(ks[2], (DEPTH, D_MODEL, IN_WIDTH), D_MODEL ** -0.5)
    base_lg = jnp.log(1.0 - 2.0 ** (-5.0 - jnp.arange(RET_HEADS, dtype=f32)))
    ret_log_gamma = base_lg[None, None, :] * (1.0 + nrm(ks[3], (DEPTH, 2, RET_HEADS), 0.05))
    n = jnp.arange(P, dtype=f32)
    ssm_a_re = -0.5 + nrm(ks[4], (DEPTH, 2, G, P), 0.01)
    ssm_a_im = math.pi * n[None, None, None, :] + nrm(ks[5], (DEPTH, 2, G, P), 0.01)
    ssm_log_dt = jax.random.uniform(ks[6], (DEPTH, 2, G), f32,
                                    math.log(DT_MIN), math.log(DT_MAX))
    ssm_b_re = nrm(ks[7], (DEPTH, 2, G, P, Hg), (2.0 * Hg) ** -0.5)
    ssm_b_im = nrm(ks[8], (DEPTH, 2, G, P, Hg), (2.0 * Hg) ** -0.5)
    ssm_c_re = nrm(ks[9], (DEPTH, 2, G, Hg, P), (2.0 * P) ** -0.5)
    ssm_c_im = nrm(ks[10], (DEPTH, 2, G, Hg, P), (2.0 * P) ** -0.5)
    ssm_d = nrm(ks[11], (DEPTH, D_MODEL), 1.0)
    w_glu = nrm(ks[12], (DEPTH, D_MODEL, D_MODEL), D_MODEL ** -0.5)
    b_glu = nrm(ks[13], (DEPTH, D_MODEL), 0.01)
    w_out = nrm(ks[14], (DEPTH, D_MODEL, D_MODEL), D_MODEL ** -0.5)
    ln_ffn_g = 1.0 + nrm(ks[15], (DEPTH, D_MODEL), 0.02)
    w_ffn_gate = nrm(ks[16], (DEPTH, D_MODEL, D_FF), D_MODEL ** -0.5)
    w_ffn_up = nrm(ks[17], (DEPTH, D_MODEL, D_FF), D_MODEL ** -0.5)
    w_ffn_down = nrm(ks[18], (DEPTH, D_FF, D_MODEL), D_FF ** -0.5)
    ln_final_g = 1.0 + nrm(ks[19], (D_MODEL,), 0.02)
    return {"x": x, "ln_mix_g": ln_mix_g, "w_in": w_in, "ret_log_gamma": ret_log_gamma,
            "ssm_a_re": ssm_a_re, "ssm_a_im": ssm_a_im, "ssm_log_dt": ssm_log_dt,
            "ssm_b_re": ssm_b_re, "ssm_b_im": ssm_b_im, "ssm_c_re": ssm_c_re,
            "ssm_c_im": ssm_c_im, "ssm_d": ssm_d, "w_glu": w_glu, "b_glu": b_glu,
            "w_out": w_out, "ln_ffn_g": ln_ffn_g, "w_ffn_gate": w_ffn_gate,
            "w_ffn_up": w_ffn_up, "w_ffn_down": w_ffn_down, "ln_final_g": ln_final_g}


def reference(x, ln_mix_g, w_in, ret_log_gamma, ssm_a_re, ssm_a_im, ssm_log_dt,
              ssm_b_re, ssm_b_im, ssm_c_re, ssm_c_im, ssm_d, w_glu, b_glu, w_out,
              ln_ffn_g, w_ffn_gate, w_ffn_up, w_ffn_down, ln_final_g):
    for i in range(DEPTH):
        h = rms_norm(x, ln_mix_g[i])
        x = x + hybrid_mixer(h, w_in[i], ret_log_gamma[i], ssm_a_re[i], ssm_a_im[i],
                             ssm_log_dt[i], ssm_b_re[i], ssm_b_im[i], ssm_c_re[i],
                             ssm_c_im[i], ssm_d[i], w_glu[i], b_glu[i], w_out[i])
        h = rms_norm(x, ln_ffn_g[i])
        x = x + swiglu(h, w_ffn_gate[i], w_ffn_up[i], w_ffn_down[i])
    return rms_norm(x, ln_final_g)
```

```python
import functools
import math

import jax
import jax.numpy as jnp
from jax import lax
from jax.experimental import pallas as pl
from jax.experimental.pallas import tpu as pltpu

F32 = jnp.float32
BF16 = jnp.bfloat16

D_MODEL = 2048
BATCH = 8
SEQ = 2048
TOKENS = BATCH * SEQ
RET_HEADS = 4
RET_QK_DIM = 256
RET_V_DIM = D_MODEL // RET_HEADS
RET_QK_WIDTH = RET_HEADS * RET_QK_DIM
ROPE_BASE = 10000.0
SSM_GROUP = 16
SSM_GROUPS = D_MODEL // SSM_GROUP
SSM_STATE = 64
D_FF = ((8 * D_MODEL // 3 + 255) // 256) * 256
IN_WIDTH = 2 * RET_QK_WIDTH + 5 * D_MODEL
EPS = 1e-6

COL_Q = 0
COL_K = RET_QK_WIDTH
COL_V = 2 * RET_QK_WIDTH
COL_G = COL_V + D_MODEL
COL_U = COL_G + D_MODEL
COL_GATE_R = COL_U + D_MODEL
COL_GATE_S = COL_GATE_R + D_MODEL

RET_CHUNK = 256
SSM_T = 16
SSM_CHUNKS = SEQ // SSM_T
SSM_ROWS = SSM_CHUNKS * BATCH
SSM_LANES = SSM_T * SSM_GROUP
SSM_GB = 4

VMEM_LIMIT = 56 * 1024 * 1024

TM = 1024
TN = 1024
GLU_TM = 512
FFN_TM = 512
FFN_TF = 512


def _params(sem):
    return pltpu.CompilerParams(dimension_semantics=sem, vmem_limit_bytes=VMEM_LIMIT)


def _rms(x, g):
    return x * lax.rsqrt(jnp.mean(x * x, axis=-1, keepdims=True) + EPS) * g


def _norm_proj_kernel(x_ref, g_ref, w_ref, o_ref, h_ref):
    @pl.when(pl.program_id(1) == 0)
    def _():
        h_ref[...] = _rms(x_ref[...], g_ref[...]).astype(BF16)

    o_ref[...] = jnp.dot(h_ref[...], w_ref[...], preferred_element_type=F32).astype(o_ref.dtype)


def _norm_proj(x, gain, w):
    m, d = x.shape
    n = w.shape[1]
    return pl.pallas_call(
        _norm_proj_kernel,
        grid=(m // TM, n // TN),
        in_specs=[pl.BlockSpec((TM, d), lambda i, j: (i, 0)),
                  pl.BlockSpec((1, d), lambda i, j: (0, 0)),
                  pl.BlockSpec((d, TN), lambda i, j: (0, j))],
        out_specs=pl.BlockSpec((TM, TN), lambda i, j: (i, j)),
        out_shape=jax.ShapeDtypeStruct((m, n), BF16),
        scratch_shapes=[pltpu.VMEM((TM, d), BF16)],
        compiler_params=_params(("parallel", "arbitrary")),
    )(x, gain, w)


def _retention_kernel(lg_ref, q_ref, k_ref, v_ref, g_ref, gr_ref, cos_ref, sin_ref, o_ref,
                      qs_ref, ks_ref, ys_ref, sf_ref, sb_ref):
    c_len = RET_CHUNK
    n_chunks = SEQ // c_len
    half = RET_QK_DIM // 2
    head = pl.program_id(1)
    lg_f = lg_ref[0, head]
    lg_b = lg_ref[1, head]

    cos = cos_ref[...]
    sin = sin_ref[...]
    q = q_ref[...].astype(F32)
    k = k_ref[...].astype(F32) * (RET_QK_DIM ** -0.5)
    qs_ref[:, :half] = q[:, :half] * cos - q[:, half:] * sin
    qs_ref[:, half:] = q[:, :half] * sin + q[:, half:] * cos
    ks_ref[:, :half] = k[:, :half] * cos - k[:, half:] * sin
    ks_ref[:, half:] = k[:, :half] * sin + k[:, half:] * cos

    t_row = lax.broadcasted_iota(jnp.int32, (c_len, RET_QK_DIM), 0).astype(F32)
    q_scale_f = jnp.exp(lg_f * (t_row + 1.0))
    q_scale_b = jnp.exp(lg_b * (c_len - t_row))
    k_scale_f = jnp.exp(lg_f * (c_len - 1.0 - t_row))
    k_scale_b = jnp.exp(lg_b * t_row)
    diff = (lax.broadcasted_iota(jnp.int32, (c_len, c_len), 0)
            - lax.broadcasted_iota(jnp.int32, (c_len, c_len), 1)).astype(F32)
    dmat = jnp.exp(jnp.where(diff >= 0, lg_f * diff, -lg_b * diff))
    decay_f = jnp.exp(jnp.full((1, RET_V_DIM), lg_f * c_len, F32))
    decay_b = jnp.exp(jnp.full((1, RET_V_DIM), lg_b * c_len, F32))

    sf_ref[...] = jnp.zeros_like(sf_ref)
    sb_ref[...] = jnp.zeros_like(sb_ref)

    def rows(c):
        return pl.ds(pl.multiple_of(c * c_len, c_len), c_len)

    def forward(c, carry):
        r = rows(c)
        qc = qs_ref[r, :]
        kc = ks_ref[r, :]
        vc = v_ref[r, :]
        scores = lax.dot_general(qc.astype(BF16), kc.astype(BF16), (((1,), (1,)), ((), ())),
                                 preferred_element_type=F32) * dmat
        y = jnp.dot(scores.astype(BF16), vc, preferred_element_type=F32)
        y = y + jnp.dot((qc * q_scale_f).astype(BF16), sf_ref[...].astype(BF16),
                        preferred_element_type=F32)
        ys_ref[r, :] = y
        kv = lax.dot_general((kc * k_scale_f).astype(BF16), vc, (((0,), (0,)), ((), ())),
                             preferred_element_type=F32)
        sf_ref[...] = decay_f * sf_ref[...] + kv
        return carry

    lax.fori_loop(0, n_chunks, forward, 0)

    def backward(i, carry):
        c = n_chunks - 1 - i
        r = rows(c)
        qc = qs_ref[r, :]
        kc = ks_ref[r, :]
        vc = v_ref[r, :]
        y = ys_ref[r, :] + jnp.dot((qc * q_scale_b).astype(BF16), sb_ref[...].astype(BF16),
                                   preferred_element_type=F32)
        kv = lax.dot_general((kc * k_scale_b).astype(BF16), vc, (((0,), (0,)), ((), ())),
                             preferred_element_type=F32)
        sb_ref[...] = decay_b * sb_ref[...] + kv
        yn = y * lax.rsqrt(jnp.mean(y * y, axis=-1, keepdims=True) + EPS)
        g = g_ref[r, :].astype(F32)
        gr = gr_ref[r, :].astype(F32)
        o_ref[r, :] = (jax.nn.sigmoid(gr) * (g * jax.nn.sigmoid(g) * yn)).astype(o_ref.dtype)
        return carry

    lax.fori_loop(0, n_chunks, backward, 0)


def _retention(proj3, log_gamma, cos, sin):
    qk_blk = RET_QK_DIM
    v_blk = RET_V_DIM
    return pl.pallas_call(
        _retention_kernel,
        grid=(BATCH, RET_HEADS),
        in_specs=[
            pl.BlockSpec(memory_space=pltpu.SMEM),
            pl.BlockSpec((None, SEQ, qk_blk), lambda b, h: (b, 0, COL_Q // qk_blk + h)),
            pl.BlockSpec((None, SEQ, qk_blk), lambda b, h: (b, 0, COL_K // qk_blk + h)),
            pl.BlockSpec((None, SEQ, v_blk), lambda b, h: (b, 0, COL_V // v_blk + h)),
            pl.BlockSpec((None, SEQ, v_blk), lambda b, h: (b, 0, COL_G // v_blk + h)),
            pl.BlockSpec((None, SEQ, v_blk), lambda b, h: (b, 0, COL_GATE_R // v_blk + h)),
            pl.BlockSpec((SEQ, RET_QK_DIM // 2), lambda b, h: (0, 0)),
            pl.BlockSpec((SEQ, RET_QK_DIM // 2), lambda b, h: (0, 0)),
        ],
        out_specs=pl.BlockSpec((None, SEQ, v_blk), lambda b, h: (b, 0, h)),
        out_shape=jax.ShapeDtypeStruct((BATCH, SEQ, D_MODEL), BF16),
        scratch_shapes=[pltpu.VMEM((SEQ, RET_QK_DIM), F32),
                        pltpu.VMEM((SEQ, RET_QK_DIM), F32),
                        pltpu.VMEM((SEQ, RET_V_DIM), F32),
                        pltpu.VMEM((RET_QK_DIM, RET_V_DIM), F32),
                        pltpu.VMEM((RET_QK_DIM, RET_V_DIM), F32)],
        compiler_params=_params(("parallel", "arbitrary")),
    )(log_gamma, proj3, proj3, proj3, proj3, proj3, cos, sin)


def _ssm_weights(a_re, a_im, log_dt, b_re, b_im, c_re, c_im):
    t_len, p, hg = SSM_T, SSM_STATE, SSM_GROUP
    hi = lax.Precision.HIGHEST
    n = jnp.arange(t_len + 1, dtype=F32)
    per_dir = []
    for d in range(2):
        dt = jnp.exp(log_dt[d])[:, None]
        lr, li = a_re[d], a_im[d]
        mag = jnp.exp(lr * dt)
        lbr, lbi = mag * jnp.cos(li * dt), mag * jnp.sin(li * dt)
        den = lr * lr + li * li
        nr, ni = lbr - 1.0, lbi
        fr, fi = (nr * lr + ni * li) / den, (ni * lr - nr * li) / den
        bbr = fr[:, :, None] * b_re[d] - fi[:, :, None] * b_im[d]
        bbi = fr[:, :, None] * b_im[d] + fi[:, :, None] * b_re[d]
        pm = jnp.exp(n[None, :, None] * (lr * dt)[:, None, :])
        pa = n[None, :, None] * (li * dt)[:, None, :]
        pwr, pwi = pm * jnp.cos(pa), pm * jnp.sin(pa)
        per_dir.append((pwr, pwi, bbr, bbi, c_re[d], c_im[d]))

    def lag_kernels(pwr, pwi, bbr, bbi, cr, ci):
        lbr = pwr[:, :t_len, :, None] * bbr[:, None] - pwi[:, :t_len, :, None] * bbi[:, None]
        lbi = pwr[:, :t_len, :, None] * bbi[:, None] + pwi[:, :t_len, :, None] * bbr[:, None]
        return (jnp.einsum('gop,gtpi->gtoi', cr, lbr, precision=hi)
                - jnp.einsum('gop,gtpi->gtoi', ci, lbi, precision=hi))

    kf = lag_kernels(*per_dir[0])
    kb = lag_kernels(*per_dir[1])
    s_idx = jnp.arange(t_len)[:, None]
    t_idx = jnp.arange(t_len)[None, :]
    lag_f = jnp.clip(t_idx - s_idx, 0, t_len - 1)
    lag_b = jnp.clip(s_idx - t_idx, 0, t_len - 1)
    m_f = jnp.where((t_idx >= s_idx)[None, :, :, None, None], kf[:, lag_f], 0.0)
    m_b = jnp.where((s_idx >= t_idx)[None, :, :, None, None], kb[:, lag_b], 0.0)
    w_intra = (m_f + m_b).transpose(0, 1, 4, 2, 3).reshape(SSM_GROUPS, SSM_LANES, SSM_LANES)

    def state_cols(pwr, pwi, bbr, bbi, powers):
        wr = pwr[:, powers, :, None] * bbr[:, None] - pwi[:, powers, :, None] * bbi[:, None]
        wi = pwr[:, powers, :, None] * bbi[:, None] + pwi[:, powers, :, None] * bbr[:, None]
        wr = wr.transpose(0, 1, 3, 2).reshape(SSM_GROUPS, SSM_LANES, p)
        wi = wi.transpose(0, 1, 3, 2).reshape(SSM_GROUPS, SSM_LANES, p)
        return jnp.concatenate([wr, wi, wi, wr], axis=-1)

    pw_f = jnp.arange(t_len - 1, -1, -1)
    pw_b = jnp.arange(t_len)
    w_state = jnp.concatenate([state_cols(*per_dir[0][:4], pw_f),
                               state_cols(*per_dir[1][:4], pw_b)], axis=-1)

    def inter_rows(pwr, pwi, cr, ci, powers):
        clr = cr[:, None] * pwr[:, powers][:, :, None, :] - ci[:, None] * pwi[:, powers][:, :, None, :]
        cli = cr[:, None] * pwi[:, powers][:, :, None, :] + ci[:, None] * pwr[:, powers][:, :, None, :]
        wr = clr.transpose(0, 3, 1, 2).reshape(SSM_GROUPS, p, SSM_LANES)
        wi = (-cli).transpose(0, 3, 1, 2).reshape(SSM_GROUPS, p, SSM_LANES)
        return jnp.concatenate([wr, wi], axis=1)

    pwo_f = jnp.arange(1, t_len + 1)
    pwo_b = jnp.arange(t_len, 0, -1)
    w_inter = jnp.concatenate([inter_rows(per_dir[0][0], per_dir[0][1], per_dir[0][4], per_dir[0][5], pwo_f),
                               inter_rows(per_dir[1][0], per_dir[1][1], per_dir[1][4], per_dir[1][5], pwo_b)],
                              axis=1)

    def scan_coeffs(pwr, pwi):
        ar, ai = pwr[:, t_len], pwi[:, t_len]
        a1 = jnp.concatenate([ar, ar], axis=-1)
        a2 = jnp.concatenate([-ai, ai], axis=-1)
        return [a1, a2, -a2]

    coeffs = jnp.stack(scan_coeffs(per_dir[0][0], per_dir[0][1])
                       + scan_coeffs(per_dir[1][0], per_dir[1][1]), axis=1)
    coeffs = jnp.broadcast_to(coeffs[:, :, None, :], (SSM_GROUPS, 6, BATCH, 2 * p))
    return w_intra.astype(BF16), w_state.astype(BF16), w_inter.astype(BF16), coeffs.astype(F32)


def _ssm_kernel(u_ref, wi_ref, ws_ref, wx_ref, a_ref, o_ref, s_ref, x_ref):
    p2 = 2 * SSM_STATE
    for g in range(SSM_GB):
        s_ref[g] = jnp.dot(u_ref[g], ws_ref[g], preferred_element_type=F32)

    def step(c, carry):
        rf = pl.ds(pl.multiple_of(c * BATCH, BATCH), BATCH)
        rb = pl.ds(pl.multiple_of((SSM_CHUNKS - 1 - c) * BATCH, BATCH), BATCH)
        new = []
        for g in range(SSM_GB):
            xf, xfs, xb, xbs = carry[4 * g:4 * g + 4]
            x_ref[g, rf, 0:p2] = xf
            x_ref[g, rb, p2:2 * p2] = xb
            nxf = a_ref[g, 0] * xf + a_ref[g, 1] * xfs + s_ref[g, rf, 0:p2]
            nxfs = a_ref[g, 0] * xfs + a_ref[g, 2] * xf + s_ref[g, rf, p2:2 * p2]
            nxb = a_ref[g, 3] * xb + a_ref[g, 4] * xbs + s_ref[g, rb, 2 * p2:3 * p2]
            nxbs = a_ref[g, 3] * xbs + a_ref[g, 5] * xb + s_ref[g, rb, 3 * p2:4 * p2]
            new += [nxf, nxfs, nxb, nxbs]
        return tuple(new)

    zero = jnp.zeros((BATCH, p2), F32)
    lax.fori_loop(0, SSM_CHUNKS, step, (zero,) * (4 * SSM_GB))

    for g in range(SSM_GB):
        y = jnp.dot(u_ref[g], wi_ref[g], preferred_element_type=F32)
        y = y + jnp.dot(x_ref[g].astype(BF16), wx_ref[g], preferred_element_type=F32)
        o_ref[g] = y.astype(o_ref.dtype)


def _ssm(u_t, w_intra, w_state, w_inter, coeffs):
    gb = SSM_GB
    return pl.pallas_call(
        _ssm_kernel,
        grid=(SSM_GROUPS // gb,),
        in_specs=[pl.BlockSpec((gb, SSM_ROWS, SSM_LANES), lambda i: (i, 0, 0)),
                  pl.BlockSpec((gb, SSM_LANES, SSM_LANES), lambda i: (i, 0, 0)),
                  pl.BlockSpec((gb, SSM_LANES, 8 * SSM_STATE), lambda i: (i, 0, 0)),
                  pl.BlockSpec((gb, 4 * SSM_STATE, SSM_LANES), lambda i: (i, 0, 0)),
                  pl.BlockSpec((gb, 6, BATCH, 2 * SSM_STATE), lambda i: (i, 0, 0, 0))],
        out_specs=pl.BlockSpec((gb, SSM_ROWS, SSM_LANES), lambda i: (i, 0, 0)),
        out_shape=jax.ShapeDtypeStruct((SSM_GROUPS, SSM_ROWS, SSM_LANES), BF16),
        scratch_shapes=[pltpu.VMEM((gb, SSM_ROWS, 8 * SSM_STATE), F32),
                        pltpu.VMEM((gb, SSM_ROWS, 4 * SSM_STATE), F32)],
        compiler_params=_params(("parallel",)),
    )(u_t, w_intra, w_state, w_inter, coeffs)


def _gelu_tanh(x):
    return 0.5 * x * (1.0 + jnp.tanh(math.sqrt(2.0 / math.pi) * (x + 0.044715 * (x * x * x))))


def _glu_merge_kernel(ys_ref, u_ref, d_ref, ysj_ref, uj_ref, dj_ref, w_ref, b_ref, gs_ref, ret_ref,
                      o_ref, a_ref):
    @pl.when(pl.program_id(1) == 0)
    def _():
        ys = ys_ref[...].astype(F32) + d_ref[...] * u_ref[...].astype(F32)
        a_ref[...] = _gelu_tanh(ys).astype(BF16)

    z = jnp.dot(a_ref[...], w_ref[...], preferred_element_type=F32) + b_ref[...]
    ysj = _gelu_tanh(ysj_ref[...].astype(F32) + dj_ref[...] * uj_ref[...].astype(F32))
    ssm_out = ysj * jax.nn.sigmoid(z)
    merged = ret_ref[...].astype(F32) + jax.nn.sigmoid(gs_ref[...].astype(F32)) * ssm_out
    o_ref[...] = merged.astype(o_ref.dtype)


def _glu_merge(ys_conv, proj, d_skip, w_glu, b_glu, ret_g):
    m, d = ys_conv.shape
    nu = COL_U // d
    tm = GLU_TM
    return pl.pallas_call(
        _glu_merge_kernel,
        grid=(m // tm, d // TN),
        in_specs=[pl.BlockSpec((tm, d), lambda i, j: (i, 0)),
                  pl.BlockSpec((tm, d), lambda i, j: (i, nu)),
                  pl.BlockSpec((1, d), lambda i, j: (0, 0)),
                  pl.BlockSpec((tm, TN), lambda i, j: (i, j)),
                  pl.BlockSpec((tm, TN), lambda i, j: (i, COL_U // TN + j)),
                  pl.BlockSpec((1, TN), lambda i, j: (0, j)),
                  pl.BlockSpec((d, TN), lambda i, j: (0, j)),
                  pl.BlockSpec((1, TN), lambda i, j: (0, j)),
                  pl.BlockSpec((tm, TN), lambda i, j: (i, COL_GATE_S // TN + j)),
                  pl.BlockSpec((tm, TN), lambda i, j: (i, j))],
        out_specs=pl.BlockSpec((tm, TN), lambda i, j: (i, j)),
        out_shape=jax.ShapeDtypeStruct((m, d), BF16),
        scratch_shapes=[pltpu.VMEM((tm, d), BF16)],
        compiler_params=_params(("parallel", "arbitrary")),
    )(ys_conv, proj, d_skip, ys_conv, proj, d_skip, w_glu, b_glu, proj, ret_g)


def _proj_residual_kernel(a_ref, w_ref, x_ref, o_ref):
    o_ref[...] = x_ref[...] + jnp.dot(a_ref[...], w_ref[...], preferred_element_type=F32)


def _proj_residual(a, w, x):
    m, kdim = a.shape
    n = w.shape[1]
    return pl.pallas_call(
        _proj_residual_kernel,
        grid=(m // TM, n // TN),
        in_specs=[pl.BlockSpec((TM, kdim), lambda i, j: (i, 0)),
                  pl.BlockSpec((kdim, TN), lambda i, j: (0, j)),
                  pl.BlockSpec((TM, TN), lambda i, j: (i, j))],
        out_specs=pl.BlockSpec((TM, TN), lambda i, j: (i, j)),
        out_shape=jax.ShapeDtypeStruct((m, n), F32),
        compiler_params=_params(("parallel", "parallel")),
    )(a, w, x)


def _ffn_kernel(x_ref, g_ref, wg_ref, wu_ref, wd_ref, gf_ref, o_ref, h_ref, acc_ref, *, final_norm):
    j = pl.program_id(1)

    @pl.when(j == 0)
    def _():
        h_ref[...] = _rms(x_ref[...], g_ref[...]).astype(BF16)
        acc_ref[...] = jnp.zeros_like(acc_ref)

    h = h_ref[...]
    gate = jnp.dot(h, wg_ref[...], preferred_element_type=F32)
    up = jnp.dot(h, wu_ref[...], preferred_element_type=F32)
    act = (gate * jax.nn.sigmoid(gate) * up).astype(BF16)
    acc_ref[...] += jnp.dot(act, wd_ref[...], preferred_element_type=F32)

    @pl.when(j == pl.num_programs(1) - 1)
    def _():
        y = x_ref[...] + acc_ref[...]
        if final_norm:
            y = _rms(y, gf_ref[...])
        o_ref[...] = y


def _ffn(x, gain, w_gate, w_up, w_down, final_gain, final_norm):
    m, d = x.shape
    f = w_gate.shape[1]
    return pl.pallas_call(
        functools.partial(_ffn_kernel, final_norm=final_norm),
        grid=(m // FFN_TM, f // FFN_TF),
        in_specs=[pl.BlockSpec((FFN_TM, d), lambda i, j: (i, 0)),
                  pl.BlockSpec((1, d), lambda i, j: (0, 0)),
                  pl.BlockSpec((d, FFN_TF), lambda i, j: (0, j)),
                  pl.BlockSpec((d, FFN_TF), lambda i, j: (0, j)),
                  pl.BlockSpec((FFN_TF, d), lambda i, j: (j, 0)),
                  pl.BlockSpec((1, d), lambda i, j: (0, 0))],
        out_specs=pl.BlockSpec((FFN_TM, d), lambda i, j: (i, 0)),
        out_shape=jax.ShapeDtypeStruct((m, d), F32),
        scratch_shapes=[pltpu.VMEM((FFN_TM, d), BF16), pltpu.VMEM((FFN_TM, d), F32)],
        compiler_params=_params(("parallel", "arbitrary")),
    )(x, gain, w_gate, w_up, w_down, final_gain)


def _to_chunked(u):
    u = u.reshape(BATCH, SSM_CHUNKS, SSM_T, SSM_GROUPS, SSM_GROUP)
    return u.transpose(3, 1, 0, 2, 4).reshape(SSM_GROUPS, SSM_ROWS, SSM_LANES)


def _from_chunked(y):
    y = y.reshape(SSM_GROUPS, SSM_CHUNKS, BATCH, SSM_T, SSM_GROUP)
    return y.transpose(2, 1, 3, 0, 4).reshape(TOKENS, D_MODEL)


def _rotary_tables():
    half = RET_QK_DIM // 2
    inv = 1.0 / (ROPE_BASE ** (jnp.arange(half, dtype=F32) / half))
    ang = jnp.arange(SEQ, dtype=F32)[:, None] * inv[None, :]
    return jnp.cos(ang), jnp.sin(ang)


def kernel(x, ln_mix_g, w_in, ret_log_gamma, ssm_a_re, ssm_a_im, ssm_log_dt, ssm_b_re, ssm_b_im,
           ssm_c_re, ssm_c_im, ssm_d, w_glu, b_glu, w_out, ln_ffn_g, w_ffn_gate, w_ffn_up,
           w_ffn_down, ln_final_g):
    depth = w_in.shape[0]
    cos, sin = _rotary_tables()
    xt = x.reshape(TOKENS, D_MODEL)
    final_gain = ln_final_g.reshape(1, D_MODEL)
    for i in range(depth):
        proj = _norm_proj(xt, ln_mix_g[i].reshape(1, D_MODEL), w_in[i].astype(BF16))
        ret_g = _retention(proj.reshape(BATCH, SEQ, IN_WIDTH), ret_log_gamma[i].astype(F32), cos, sin)
        ssm_w = _ssm_weights(ssm_a_re[i], ssm_a_im[i], ssm_log_dt[i], ssm_b_re[i], ssm_b_im[i],
                             ssm_c_re[i], ssm_c_im[i])
        ys_conv = _from_chunked(_ssm(_to_chunked(proj[:, COL_U:COL_U + D_MODEL]), *ssm_w))
        merged = _glu_merge(ys_conv, proj, ssm_d[i].reshape(1, D_MODEL), w_glu[i].astype(BF16),
                            b_glu[i].reshape(1, D_MODEL), ret_g.reshape(TOKENS, D_MODEL))
        xt = _proj_residual(merged, w_out[i].astype(BF16), xt)
        xt = _ffn(xt, ln_ffn_g[i].reshape(1, D_MODEL), w_ffn_gate[i].astype(BF16),
                  w_ffn_up[i].astype(BF16), w_ffn_down[i].astype(BF16), final_gain,
                  final_norm=(i == depth - 1))
    return xt.reshape(BATCH, SEQ, D_MODEL)
```

```python
import functools
import math

import numpy as np
import jax
import jax.numpy as jnp
from jax import lax
from jax.experimental import pallas as pl
from jax.experimental.pallas import tpu as pltpu

F32 = jnp.float32
BF16 = jnp.bfloat16

D_MODEL = 2048
BATCH = 8
SEQ = 2048
TOKENS = BATCH * SEQ
RET_HEADS = 4
RET_QK_DIM = 256
RET_V_DIM = D_MODEL // RET_HEADS
RET_QK_WIDTH = RET_HEADS * RET_QK_DIM
ROPE_BASE = 10000.0
SSM_GROUP = 16
SSM_GROUPS = D_MODEL // SSM_GROUP
SSM_STATE = 64
D_FF = ((8 * D_MODEL // 3 + 255) // 256) * 256
IN_WIDTH = 2 * RET_QK_WIDTH + 5 * D_MODEL
EPS = 1e-6

COL_Q = 0
COL_K = RET_QK_WIDTH
COL_V = 2 * RET_QK_WIDTH
COL_G = COL_V + D_MODEL
COL_U = COL_G + D_MODEL
COL_GATE_R = COL_U + D_MODEL
COL_GATE_S = COL_GATE_R + D_MODEL

LANES = 128
SUBLANES = 8
RET_CHUNK = 256
SSM_T = 16
SSM_CHUNKS = SEQ // SSM_T
SSM_ROWS = SSM_CHUNKS * BATCH
SSM_LANES = SSM_T * SSM_GROUP
SSM_GB = 4
GROUPS_PER_VREG = LANES // SSM_GROUP
BLOCKS_PER_VREG = LANES // SSM_GROUP

VMEM_LIMIT = 56 * 1024 * 1024

TM = 1024
TN = 1024
GLU_TM = 512
FFN_TM = 512
FFN_TF = 512


def _params(sem):
    return pltpu.CompilerParams(dimension_semantics=sem, vmem_limit_bytes=VMEM_LIMIT)


def _rms(x, g):
    return x * lax.rsqrt(jnp.mean(x * x, axis=-1, keepdims=True) + EPS) * g


def _norm_proj_kernel(x_ref, g_ref, w_ref, o_ref, u_ref, h_ref):
    j = pl.program_id(1)

    @pl.when(j == 0)
    def _():
        h_ref[...] = _rms(x_ref[...], g_ref[...]).astype(BF16)

    acc = jnp.dot(h_ref[...], w_ref[...], preferred_element_type=F32)
    o_ref[...] = acc.astype(o_ref.dtype)

    @pl.when((j >= COL_U // TN) & (j < COL_GATE_R // TN))
    def _():
        u_ref[...] = acc


def _norm_proj(x, gain, w):
    m, d = x.shape
    n = w.shape[1]
    u_lo, u_hi = COL_U // TN, COL_GATE_R // TN - 1
    return pl.pallas_call(
        _norm_proj_kernel,
        grid=(m // TM, n // TN),
        in_specs=[pl.BlockSpec((TM, d), lambda i, j: (i, 0)),
                  pl.BlockSpec((1, d), lambda i, j: (0, 0)),
                  pl.BlockSpec((d, TN), lambda i, j: (0, j))],
        out_specs=[pl.BlockSpec((TM, TN), lambda i, j: (i, j)),
                   pl.BlockSpec((TM, TN), lambda i, j: (i, jnp.clip(j, u_lo, u_hi) - u_lo))],
        out_shape=[jax.ShapeDtypeStruct((m, n), BF16),
                   jax.ShapeDtypeStruct((m, D_MODEL), F32)],
        scratch_shapes=[pltpu.VMEM((TM, d), BF16)],
        compiler_params=_params(("parallel", "arbitrary")),
    )(x, gain, w)


def _retention_kernel(lg_ref, q_ref, k_ref, v_ref, g_ref, gr_ref, cos_ref, sin_ref, o_ref,
                      qs_ref, ks_ref, ys_ref, sf_ref, sb_ref):
    c_len = RET_CHUNK
    n_chunks = SEQ // c_len
    half = RET_QK_DIM // 2
    head = pl.program_id(1)
    lg_f = lg_ref[0, head]
    lg_b = lg_ref[1, head]

    cos = cos_ref[...]
    sin = sin_ref[...]
    q = q_ref[...].astype(F32)
    k = k_ref[...].astype(F32) * (RET_QK_DIM ** -0.5)
    qs_ref[:, :half] = q[:, :half] * cos - q[:, half:] * sin
    qs_ref[:, half:] = q[:, :half] * sin + q[:, half:] * cos
    ks_ref[:, :half] = k[:, :half] * cos - k[:, half:] * sin
    ks_ref[:, half:] = k[:, :half] * sin + k[:, half:] * cos

    t_row = lax.broadcasted_iota(jnp.int32, (c_len, RET_QK_DIM), 0).astype(F32)
    q_scale_f = jnp.exp(lg_f * (t_row + 1.0))
    q_scale_b = jnp.exp(lg_b * (c_len - t_row))
    k_scale_f = jnp.exp(lg_f * (c_len - 1.0 - t_row))
    k_scale_b = jnp.exp(lg_b * t_row)
    diff = (lax.broadcasted_iota(jnp.int32, (c_len, c_len), 0)
            - lax.broadcasted_iota(jnp.int32, (c_len, c_len), 1)).astype(F32)
    dmat = jnp.exp(jnp.where(diff >= 0, lg_f * diff, -lg_b * diff))
    decay_f = jnp.exp(jnp.full((1, RET_V_DIM), lg_f * c_len, F32))
    decay_b = jnp.exp(jnp.full((1, RET_V_DIM), lg_b * c_len, F32))

    sf_ref[...] = jnp.zeros_like(sf_ref)
    sb_ref[...] = jnp.zeros_like(sb_ref)

    def rows(c):
        return pl.ds(pl.multiple_of(c * c_len, c_len), c_len)

    def forward(c, carry):
        r = rows(c)
        qc = qs_ref[r, :]
        kc = ks_ref[r, :]
        vc = v_ref[r, :]
        scores = lax.dot_general(qc.astype(BF16), kc.astype(BF16), (((1,), (1,)), ((), ())),
                                 preferred_element_type=F32) * dmat
        y = jnp.dot(scores.astype(BF16), vc, preferred_element_type=F32)
        y = y + jnp.dot((qc * q_scale_f).astype(BF16), sf_ref[...].astype(BF16),
                        preferred_element_type=F32)
        ys_ref[r, :] = y
        kv = lax.dot_general((kc * k_scale_f).astype(BF16), vc, (((0,), (0,)), ((), ())),
                             preferred_element_type=F32)
        sf_ref[...] = decay_f * sf_ref[...] + kv
        return carry

    lax.fori_loop(0, n_chunks, forward, 0)

    def backward(i, carry):
        c = n_chunks - 1 - i
        r = rows(c)
        qc = qs_ref[r, :]
        kc = ks_ref[r, :]
        vc = v_ref[r, :]
        y = ys_ref[r, :] + jnp.dot((qc * q_scale_b).astype(BF16), sb_ref[...].astype(BF16),
                                   preferred_element_type=F32)
        kv = lax.dot_general((kc * k_scale_b).astype(BF16), vc, (((0,), (0,)), ((), ())),
                             preferred_element_type=F32)
        sb_ref[...] = decay_b * sb_ref[...] + kv
        yn = y * lax.rsqrt(jnp.mean(y * y, axis=-1, keepdims=True) + EPS)
        g = g_ref[r, :].astype(F32)
        gr = gr_ref[r, :].astype(F32)
        o_ref[r, :] = (jax.nn.sigmoid(gr) * (g * jax.nn.sigmoid(g) * yn)).astype(o_ref.dtype)
        return carry

    lax.fori_loop(0, n_chunks, backward, 0)


def _retention(proj3, log_gamma, cos, sin):
    qk_blk = RET_QK_DIM
    v_blk = RET_V_DIM
    return pl.pallas_call(
        _retention_kernel,
        grid=(BATCH, RET_HEADS),
        in_specs=[
            pl.BlockSpec(memory_space=pltpu.SMEM),
            pl.BlockSpec((None, SEQ, qk_blk), lambda b, h: (b, 0, COL_Q // qk_blk + h)),
            pl.BlockSpec((None, SEQ, qk_blk), lambda b, h: (b, 0, COL_K // qk_blk + h)),
            pl.BlockSpec((None, SEQ, v_blk), lambda b, h: (b, 0, COL_V // v_blk + h)),
            pl.BlockSpec((None, SEQ, v_blk), lambda b, h: (b, 0, COL_G // v_blk + h)),
            pl.BlockSpec((None, SEQ, v_blk), lambda b, h: (b, 0, COL_GATE_R // v_blk + h)),
            pl.BlockSpec((SEQ, RET_QK_DIM // 2), lambda b, h: (0, 0)),
            pl.BlockSpec((SEQ, RET_QK_DIM // 2), lambda b, h: (0, 0)),
        ],
        out_specs=pl.BlockSpec((None, SEQ, v_blk), lambda b, h: (b, 0, h)),
        out_shape=jax.ShapeDtypeStruct((BATCH, SEQ, D_MODEL), BF16),
        scratch_shapes=[pltpu.VMEM((SEQ, RET_QK_DIM), F32),
                        pltpu.VMEM((SEQ, RET_QK_DIM), F32),
                        pltpu.VMEM((SEQ, RET_V_DIM), F32),
                        pltpu.VMEM((RET_QK_DIM, RET_V_DIM), F32),
                        pltpu.VMEM((RET_QK_DIM, RET_V_DIM), F32)],
        compiler_params=_params(("parallel", "arbitrary")),
    )(log_gamma, proj3, proj3, proj3, proj3, proj3, cos, sin)


def _ssm_operators(a_re, a_im, log_dt, b_re, b_im, c_re, c_im):
    t_len, p = SSM_T, SSM_STATE
    n_state = (np.arange(t_len - 1, -1, -1, dtype=np.float32), np.arange(t_len, dtype=np.float32))
    n_out = (np.arange(1, t_len + 1, dtype=np.float32), np.arange(t_len, 0, -1, dtype=np.float32))
    state_parts, inter_parts, coeff_rows, s_cat, c_tile = [], [], [], [], []
    for d in range(2):
        dt = jnp.exp(log_dt[d])[:, None]
        lr, li = a_re[d], a_im[d]
        lrdt, lidt = lr * dt, li * dt

        def powers(n):
            mag = jnp.exp(n[None, :, None] * lrdt[:, None, :])
            ang = n[None, :, None] * lidt[:, None, :]
            return mag * jnp.cos(ang), mag * jnp.sin(ang)

        (lbr, lbi) = [v[:, 0] for v in powers(np.ones((1,), np.float32))]
        den = lr * lr + li * li
        nr, ni = lbr - 1.0, lbi
        fr, fi = (nr * lr + ni * li) / den, (ni * lr - nr * li) / den
        bt_r = jnp.swapaxes(b_re[d], 1, 2)
        bt_i = jnp.swapaxes(b_im[d], 1, 2)
        bbr = fr[:, None, :] * bt_r - fi[:, None, :] * bt_i
        bbi = fr[:, None, :] * bt_i + fi[:, None, :] * bt_r

        pr, pi = powers(n_state[d])
        s_r = (pr[:, :, None, :] * bbr[:, None] - pi[:, :, None, :] * bbi[:, None]
               ).reshape(SSM_GROUPS, SSM_LANES, p)
        s_i = (pr[:, :, None, :] * bbi[:, None] + pi[:, :, None, :] * bbr[:, None]
               ).reshape(SSM_GROUPS, SSM_LANES, p)
        state_parts += [s_r, s_i]
        s_cat.append(jnp.concatenate([s_r, s_i], axis=-1))

        cr, ci = c_re[d], c_im[d]
        qr, qi = powers(n_out[d])
        i_r = (cr[:, None] * qr[:, :, None, :] - ci[:, None] * qi[:, :, None, :]
               ).reshape(SSM_GROUPS, SSM_LANES, p)
        i_i = (cr[:, None] * qi[:, :, None, :] + ci[:, None] * qr[:, :, None, :]
               ).reshape(SSM_GROUPS, SSM_LANES, p)
        inter_parts += [i_r, -i_i]
        cc = jnp.concatenate([cr, -ci], axis=-1)
        c_tile.append(jnp.broadcast_to(cc[:, None], (SSM_GROUPS, t_len, SSM_GROUP, 2 * p)
                                       ).reshape(SSM_GROUPS, SSM_LANES, 2 * p))

        ar, ai = [v[:, 0] for v in powers(np.full((1,), t_len, np.float32))]
        a1 = jnp.concatenate([ar, ar], axis=-1)
        a2 = jnp.concatenate([-ai, ai], axis=-1)
        coeff_rows += [a1, a2, -a2]

    w_state = jnp.concatenate(state_parts, axis=-1).astype(BF16)
    w_inter_t = jnp.concatenate(inter_parts, axis=-1).astype(BF16)
    coeffs = jnp.stack(coeff_rows, axis=1)
    coeffs = jnp.broadcast_to(coeffs[:, :, None, :], (SSM_GROUPS, 6, BATCH, 2 * p)).astype(F32)
    return w_state, w_inter_t, coeffs, s_cat, c_tile


def _ssm_intra_kernel(sf_ref, cf_ref, sb_ref, cb_ref, o_ref):
    nt = (((1,), (1,)), ((), ()))
    blk = lax.broadcasted_iota(jnp.int32, (SSM_LANES, SSM_LANES), 1) // SSM_GROUP
    for g in range(o_ref.shape[0]):
        kf = lax.dot_general(sf_ref[g], cf_ref[g], nt, precision=lax.Precision.HIGHEST,
                             preferred_element_type=F32)
        kb = lax.dot_general(sb_ref[g], cb_ref[g], nt, precision=lax.Precision.HIGHEST,
                             preferred_element_type=F32)
        acc = jnp.zeros((SSM_LANES, SSM_LANES), F32)
        for t in range(SSM_T):
            up = (SSM_T - 1 - t) * SSM_GROUP
            down = t * SSM_GROUP
            parts_f = [kf[up:, :]] + ([jnp.zeros((up, SSM_LANES), F32)] if up else [])
            parts_b = ([jnp.zeros((down, SSM_LANES), F32)] if down else []) + [kb[:SSM_LANES - down, :]]
            shifted = jnp.concatenate(parts_f, axis=0) + jnp.concatenate(parts_b, axis=0)
            acc = jnp.where(blk == t, shifted, acc)
        o_ref[g] = acc.astype(o_ref.dtype)


def _ssm_intra(s_cat, c_tile):
    gb = 8
    spec = pl.BlockSpec((gb, SSM_LANES, 2 * SSM_STATE), lambda i: (i, 0, 0))
    return pl.pallas_call(
        _ssm_intra_kernel,
        grid=(SSM_GROUPS // gb,),
        in_specs=[spec, spec, spec, spec],
        out_specs=pl.BlockSpec((gb, SSM_LANES, SSM_LANES), lambda i: (i, 0, 0)),
        out_shape=jax.ShapeDtypeStruct((SSM_GROUPS, SSM_LANES, SSM_LANES), BF16),
        compiler_params=_params(("parallel",)),
    )(s_cat[0], c_tile[0], s_cat[1], c_tile[1])


def _lane_block_ids():
    return lax.broadcasted_iota(jnp.int32, (BATCH, LANES), 1) // SSM_GROUP


def _ssm_pack_kernel(u_ref, o_ref):
    blk = _lane_block_ids()

    def body(cp, carry):
        halves = []
        for k in range(2):
            c = 2 * cp + k
            v = [u_ref[pl.ds(c * SSM_T + s, BATCH, stride=SEQ), :] for s in range(SSM_T)]
            per_group = []
            for gl in range(GROUPS_PER_VREG):
                parts = []
                for s_hi in range(SSM_T // BLOCKS_PER_VREG):
                    acc = None
                    for s_lo in range(BLOCKS_PER_VREG):
                        src = v[s_hi * BLOCKS_PER_VREG + s_lo]
                        shift = ((s_lo - gl) * SSM_GROUP) % LANES
                        r = pltpu.roll(src, shift, axis=1) if shift else src
                        acc = r if acc is None else jnp.where(blk == s_lo, r, acc)
                    parts.append(acc)
                per_group.append(jnp.concatenate(parts, axis=1))
            halves.append(per_group)
        rows = pl.ds(pl.multiple_of(cp * 2 * BATCH, 2 * BATCH), 2 * BATCH)
        for gl in range(GROUPS_PER_VREG):
            o_ref[gl, rows, :] = jnp.concatenate([halves[0][gl], halves[1][gl]], axis=0).astype(BF16)
        return carry

    lax.fori_loop(0, SSM_CHUNKS // 2, body, 0)


def _ssm_pack(u):
    return pl.pallas_call(
        _ssm_pack_kernel,
        grid=(D_MODEL // LANES,),
        in_specs=[pl.BlockSpec((TOKENS, LANES), lambda i: (0, i))],
        out_specs=pl.BlockSpec((GROUPS_PER_VREG, SSM_ROWS, SSM_LANES), lambda i: (i, 0, 0)),
        out_shape=jax.ShapeDtypeStruct((SSM_GROUPS, SSM_ROWS, SSM_LANES), BF16),
        compiler_params=_params(("parallel",)),
    )(u)


def _ssm_unpack_kernel(y_ref, o_ref):
    blk = _lane_block_ids()

    def body(cp, carry):
        rows = pl.ds(pl.multiple_of(cp * 2 * BATCH, 2 * BATCH), 2 * BATCH)
        y = [y_ref[gl, rows, :].astype(F32) for gl in range(GROUPS_PER_VREG)]
        for k in range(2):
            c = 2 * cp + k
            for t_hi in range(SSM_T // BLOCKS_PER_VREG):
                src = [y[gl][k * BATCH:(k + 1) * BATCH, t_hi * LANES:(t_hi + 1) * LANES]
                       for gl in range(GROUPS_PER_VREG)]
                for t_lo in range(BLOCKS_PER_VREG):
                    acc = None
                    for gl in range(GROUPS_PER_VREG):
                        shift = ((gl - t_lo) * SSM_GROUP) % LANES
                        r = pltpu.roll(src[gl], shift, axis=1) if shift else src[gl]
                        acc = r if acc is None else jnp.where(blk == gl, r, acc)
                    t = t_hi * BLOCKS_PER_VREG + t_lo
                    o_ref[pl.ds(c * SSM_T + t, BATCH, stride=SEQ), :] = acc
        return carry

    lax.fori_loop(0, SSM_CHUNKS // 2, body, 0)


def _ssm_unpack(y_t):
    return pl.pallas_call(
        _ssm_unpack_kernel,
        grid=(D_MODEL // LANES,),
        in_specs=[pl.BlockSpec((GROUPS_PER_VREG, SSM_ROWS, SSM_LANES), lambda i: (i, 0, 0))],
        out_specs=pl.BlockSpec((TOKENS, LANES), lambda i: (0, i)),
        out_shape=jax.ShapeDtypeStruct((TOKENS, D_MODEL), F32),
        compiler_params=_params(("parallel",)),
    )(y_t)


def _ssm_kernel(u_ref, wi_ref, ws_ref, wx_ref, a_ref, o_ref, s_ref, x_ref):
    p2 = 2 * SSM_STATE
    nt = (((1,), (1,)), ((), ()))
    for g in range(SSM_GB):
        s_ref[g] = jnp.dot(u_ref[g], ws_ref[g], preferred_element_type=F32)

    def step(c, carry):
        rf = pl.ds(pl.multiple_of(c * BATCH, BATCH), BATCH)
        rb = pl.ds(pl.multiple_of((SSM_CHUNKS - 1 - c) * BATCH, BATCH), BATCH)
        new = []
        for g in range(SSM_GB):
            xf, xfs, xb, xbs = carry[4 * g:4 * g + 4]
            x_ref[g, rf, 0:p2] = xf
            x_ref[g, rb, p2:2 * p2] = xb
            sf = s_ref[g, rf, 0:p2]
            sb = s_ref[g, rb, p2:2 * p2]
            nxf = a_ref[g, 0] * xf + a_ref[g, 1] * xfs + sf
            nxfs = a_ref[g, 0] * xfs + a_ref[g, 2] * xf + pltpu.roll(sf, SSM_STATE, axis=1)
            nxb = a_ref[g, 3] * xb + a_ref[g, 4] * xbs + sb
            nxbs = a_ref[g, 3] * xbs + a_ref[g, 5] * xb + pltpu.roll(sb, SSM_STATE, axis=1)
            new += [nxf, nxfs, nxb, nxbs]
        return tuple(new)

    zero = jnp.zeros((BATCH, p2), F32)
    lax.fori_loop(0, SSM_CHUNKS, step, (zero,) * (4 * SSM_GB))

    for g in range(SSM_GB):
        y = jnp.dot(u_ref[g], wi_ref[g], preferred_element_type=F32)
        y = y + lax.dot_general(x_ref[g].astype(BF16), wx_ref[g], nt, preferred_element_type=F32)
        o_ref[g] = y.astype(o_ref.dtype)


def _ssm(u_t, w_intra, w_state, w_inter_t, coeffs):
    gb = SSM_GB
    wspec = pl.BlockSpec((gb, SSM_LANES, SSM_LANES), lambda i: (i, 0, 0))
    return pl.pallas_call(
        _ssm_kernel,
        grid=(SSM_GROUPS // gb,),
        in_specs=[pl.BlockSpec((gb, SSM_ROWS, SSM_LANES), lambda i: (i, 0, 0)),
                  wspec, wspec, wspec,
                  pl.BlockSpec((gb, 6, BATCH, 2 * SSM_STATE), lambda i: (i, 0, 0, 0))],
        out_specs=pl.BlockSpec((gb, SSM_ROWS, SSM_LANES), lambda i: (i, 0, 0)),
        out_shape=jax.ShapeDtypeStruct((SSM_GROUPS, SSM_ROWS, SSM_LANES), BF16),
        scratch_shapes=[pltpu.VMEM((gb, SSM_ROWS, 4 * SSM_STATE), F32),
                        pltpu.VMEM((gb, SSM_ROWS, 4 * SSM_STATE), F32)],
        compiler_params=_params(("parallel",)),
    )(u_t, w_intra, w_state, w_inter_t, coeffs)


def _gelu_tanh(x):
    return 0.5 * x * (1.0 + jnp.tanh(math.sqrt(2.0 / math.pi) * (x + 0.044715 * (x * x * x))))


def _glu_merge_kernel(ys_ref, u_ref, d_ref, ysj_ref, uj_ref, dj_ref, w_ref, b_ref, gs_ref, ret_ref,
                      o_ref, a_ref):
    @pl.when(pl.program_id(1) == 0)
    def _():
        a_ref[...] = _gelu_tanh(ys_ref[...] + d_ref[...] * u_ref[...]).astype(BF16)

    z = jnp.dot(a_ref[...], w_ref[...], preferred_element_type=F32) + b_ref[...]
    ysj = _gelu_tanh(ysj_ref[...] + dj_ref[...] * uj_ref[...])
    ssm_out = ysj * jax.nn.sigmoid(z)
    merged = ret_ref[...].astype(F32) + jax.nn.sigmoid(gs_ref[...].astype(F32)) * ssm_out
    o_ref[...] = merged.astype(o_ref.dtype)


def _glu_merge(ys_conv, u, proj, d_skip, w_glu, b_glu, ret_g):
    m, d = ys_conv.shape
    tm = GLU_TM
    row = pl.BlockSpec((tm, d), lambda i, j: (i, 0))
    tile = pl.BlockSpec((tm, TN), lambda i, j: (i, j))
    return pl.pallas_call(
        _glu_merge_kernel,
        grid=(m // tm, d // TN),
        in_specs=[row, row,
                  pl.BlockSpec((1, d), lambda i, j: (0, 0)),
                  tile, tile,
                  pl.BlockSpec((1, TN), lambda i, j: (0, j)),
                  pl.BlockSpec((d, TN), lambda i, j: (0, j)),
                  pl.BlockSpec((1, TN), lambda i, j: (0, j)),
                  pl.BlockSpec((tm, TN), lambda i, j: (i, COL_GATE_S // TN + j)),
                  tile],
        out_specs=tile,
        out_shape=jax.ShapeDtypeStruct((m, d), BF16),
        scratch_shapes=[pltpu.VMEM((tm, d), BF16)],
        compiler_params=_params(("parallel", "arbitrary")),
    )(ys_conv, u, d_skip, ys_conv, u, d_skip, w_glu, b_glu, proj, ret_g)


def _proj_residual_kernel(a_ref, w_ref, x_ref, o_ref):
    o_ref[...] = x_ref[...] + jnp.dot(a_ref[...], w_ref[...], preferred_element_type=F32)


def _proj_residual(a, w, x):
    m, kdim = a.shape
    n = w.shape[1]
    return pl.pallas_call(
        _proj_residual_kernel,
        grid=(m // TM, n // TN),
        in_specs=[pl.BlockSpec((TM, kdim), lambda i, j: (i, 0)),
                  pl.BlockSpec((kdim, TN), lambda i, j: (0, j)),
                  pl.BlockSpec((TM, TN), lambda i, j: (i, j))],
        out_specs=pl.BlockSpec((TM, TN), lambda i, j: (i, j)),
        out_shape=jax.ShapeDtypeStruct((m, n), F32),
        compiler_params=_params(("parallel", "parallel")),
    )(a, w, x)


def _ffn_kernel(x_ref, g_ref, wg_ref, wu_ref, wd_ref, gf_ref, o_ref, h_ref, acc_ref, *, final_norm):
    j = pl.program_id(1)

    @pl.when(j == 0)
    def _():
        h_ref[...] = _rms(x_ref[...], g_ref[...]).astype(BF16)
        acc_ref[...] = jnp.zeros_like(acc_ref)

    h = h_ref[...]
    gate = jnp.dot(h, wg_ref[...], preferred_element_type=F32)
    up = jnp.dot(h, wu_ref[...], preferred_element_type=F32)
    act = (gate * jax.nn.sigmoid(gate) * up).astype(BF16)
    acc_ref[...] += jnp.dot(act, wd_ref[...], preferred_element_type=F32)

    @pl.when(j == pl.num_programs(1) - 1)
    def _():
        y = x_ref[...] + acc_ref[...]
        if final_norm:
            y = _rms(y, gf_ref[...])
        o_ref[...] = y


def _ffn(x, gain, w_gate, w_up, w_down, final_gain, final_norm):
    m, d = x.shape
    f = w_gate.shape[1]
    return pl.pallas_call(
        functools.partial(_ffn_kernel, final_norm=final_norm),
        grid=(m // FFN_TM, f // FFN_TF),
        in_specs=[pl.BlockSpec((FFN_TM, d), lambda i, j: (i, 0)),
                  pl.BlockSpec((1, d), lambda i, j: (0, 0)),
                  pl.BlockSpec((d, FFN_TF), lambda i, j: (0, j)),
                  pl.BlockSpec((d, FFN_TF), lambda i, j: (0, j)),
                  pl.BlockSpec((FFN_TF, d), lambda i, j: (j, 0)),
                  pl.BlockSpec((1, d), lambda i, j: (0, 0))],
        out_specs=pl.BlockSpec((FFN_TM, d), lambda i, j: (i, 0)),
        out_shape=jax.ShapeDtypeStruct((m, d), F32),
        scratch_shapes=[pltpu.VMEM((FFN_TM, d), BF16), pltpu.VMEM((FFN_TM, d), F32)],
        compiler_params=_params(("parallel", "arbitrary")),
    )(x, gain, w_gate, w_up, w_down, final_gain)


def _rotary_tables():
    half = RET_QK_DIM // 2
    inv = 1.0 / (ROPE_BASE ** (jnp.arange(half, dtype=F32) / half))
    ang = jnp.arange(SEQ, dtype=F32)[:, None] * inv[None, :]
    return jnp.cos(ang), jnp.sin(ang)


def kernel(x, ln_mix_g, w_in, ret_log_gamma, ssm_a_re, ssm_a_im, ssm_log_dt, ssm_b_re, ssm_b_im,
           ssm_c_re, ssm_c_im, ssm_d, w_glu, b_glu, w_out, ln_ffn_g, w_ffn_gate, w_ffn_up,
           w_ffn_down, ln_final_g):
    depth = w_in.shape[0]
    cos, sin = _rotary_tables()
    xt = x.reshape(TOKENS, D_MODEL)
    final_gain = ln_final_g.reshape(1, D_MODEL)
    for i in range(depth):
        proj, u = _norm_proj(xt, ln_mix_g[i].reshape(1, D_MODEL), w_in[i].astype(BF16))
        ret_g = _retention(proj.reshape(BATCH, SEQ, IN_WIDTH), ret_log_gamma[i].astype(F32), cos, sin)
        w_state, w_inter_t, coeffs, s_cat, c_tile = _ssm_operators(
            ssm_a_re[i], ssm_a_im[i], ssm_log_dt[i], ssm_b_re[i], ssm_b_im[i], ssm_c_re[i], ssm_c_im[i])
        w_intra = _ssm_intra(s_cat, c_tile)
        ys_conv = _ssm_unpack(_ssm(_ssm_pack(u), w_intra, w_state, w_inter_t, coeffs))
        merged = _glu_merge(ys_conv, u, proj, ssm_d[i].reshape(1, D_MODEL), w_glu[i].astype(BF16),
                            b_glu[i].reshape(1, D_MODEL), ret_g.reshape(TOKENS, D_MODEL))
        xt = _proj_residual(merged, w_out[i].astype(BF16), xt)
        xt = _ffn(xt, ln_ffn_g[i].reshape(1, D_MODEL), w_ffn_gate[i].astype(BF16),
                  w_ffn_up[i].astype(BF16), w_ffn_down[i].astype(BF16), final_gain,
                  final_norm=(i == depth - 1))
    return xt.reshape(BATCH, SEQ, D_MODEL)
```

```python
import functools
import math

import numpy as np
import jax
import jax.numpy as jnp
from jax import lax
from jax.experimental import pallas as pl
from jax.experimental.pallas import tpu as pltpu

F32 = jnp.float32
BF16 = jnp.bfloat16

D_MODEL = 2048
BATCH = 8
SEQ = 2048
TOKENS = BATCH * SEQ
RET_HEADS = 4
RET_QK_DIM = 256
RET_V_DIM = D_MODEL // RET_HEADS
RET_QK_WIDTH = RET_HEADS * RET_QK_DIM
ROPE_BASE = 10000.0
SSM_GROUP = 16
SSM_GROUPS = D_MODEL // SSM_GROUP
SSM_STATE = 64
D_FF = ((8 * D_MODEL // 3 + 255) // 256) * 256
IN_WIDTH = 2 * RET_QK_WIDTH + 5 * D_MODEL
EPS = 1e-6

COL_Q = 0
COL_K = RET_QK_WIDTH
COL_V = 2 * RET_QK_WIDTH
COL_G = COL_V + D_MODEL
COL_U = COL_G + D_MODEL
COL_GATE_R = COL_U + D_MODEL
COL_GATE_S = COL_GATE_R + D_MODEL

LANES = 128
SUBLANES = 8
RET_CHUNK = 256
SSM_T = 16
SSM_CHUNKS = SEQ // SSM_T
SSM_ROWS = SSM_CHUNKS * BATCH
SSM_LANES = SSM_T * SSM_GROUP
SSM_GB = 4
GROUPS_PER_VREG = LANES // SSM_GROUP
BLOCKS_PER_VREG = LANES // SSM_GROUP

VMEM_LIMIT = 56 * 1024 * 1024

TM = 1024
PROJ_TL = 128
TN = 1024
GLU_TL = 64
GLU_TN = 512
PACK_CHUNKS = 4
FFN_TM = 512
FFN_TF = 512


def _params(sem):
    return pltpu.CompilerParams(dimension_semantics=sem, vmem_limit_bytes=VMEM_LIMIT)


def _rms(x, g):
    return x * lax.rsqrt(jnp.mean(x * x, axis=-1, keepdims=True) + EPS) * g


def _norm_proj_kernel(x_ref, g_ref, w_ref, o_ref, u_ref, h_ref):
    j = pl.program_id(1)
    rows = BATCH * PROJ_TL

    @pl.when(j == 0)
    def _():
        h_ref[...] = _rms(x_ref[...].reshape(rows, D_MODEL), g_ref[...]).astype(BF16)

    acc = jnp.dot(h_ref[...], w_ref[...], preferred_element_type=F32)
    o_ref[...] = acc.astype(o_ref.dtype).reshape(BATCH, PROJ_TL, TN)

    @pl.when((j >= COL_U // TN) & (j < COL_GATE_R // TN))
    def _():
        for k in range(TN // LANES):
            for b in range(BATCH):
                u_ref[k, pl.ds(b, PROJ_TL, stride=BATCH), :] = (
                    acc[b * PROJ_TL:(b + 1) * PROJ_TL, k * LANES:(k + 1) * LANES])


def _norm_proj(x3, gain, w):
    n = w.shape[1]
    u_lo, u_hi = COL_U // TN, COL_GATE_R // TN - 1
    rows = BATCH * PROJ_TL
    return pl.pallas_call(
        _norm_proj_kernel,
        grid=(SEQ // PROJ_TL, n // TN),
        in_specs=[pl.BlockSpec((BATCH, PROJ_TL, D_MODEL), lambda i, j: (0, i, 0)),
                  pl.BlockSpec((1, D_MODEL), lambda i, j: (0, 0)),
                  pl.BlockSpec((D_MODEL, TN), lambda i, j: (0, j))],
        out_specs=[pl.BlockSpec((BATCH, PROJ_TL, TN), lambda i, j: (0, i, j)),
                   pl.BlockSpec((TN // LANES, rows, LANES),
                                lambda i, j: (jnp.clip(j, u_lo, u_hi) - u_lo, i, 0))],
        out_shape=[jax.ShapeDtypeStruct((BATCH, SEQ, n), BF16),
                   jax.ShapeDtypeStruct((D_MODEL // LANES, TOKENS, LANES), F32)],
        scratch_shapes=[pltpu.VMEM((rows, D_MODEL), BF16)],
        compiler_params=_params(("parallel", "arbitrary")),
        name="norm_in_proj",
    )(x3, gain, w)


def _retention_kernel(lg_ref, q_ref, k_ref, v_ref, g_ref, gr_ref, cos_ref, sin_ref, o_ref,
                      qs_ref, ks_ref, ys_ref, sf_ref, sb_ref):
    c_len = RET_CHUNK
    n_chunks = SEQ // c_len
    half = RET_QK_DIM // 2
    head = pl.program_id(1)
    lg_f = lg_ref[0, head]
    lg_b = lg_ref[1, head]

    cos = cos_ref[...]
    sin = sin_ref[...]
    q = q_ref[...].astype(F32)
    k = k_ref[...].astype(F32) * (RET_QK_DIM ** -0.5)
    qs_ref[:, :half] = q[:, :half] * cos - q[:, half:] * sin
    qs_ref[:, half:] = q[:, :half] * sin + q[:, half:] * cos
    ks_ref[:, :half] = k[:, :half] * cos - k[:, half:] * sin
    ks_ref[:, half:] = k[:, :half] * sin + k[:, half:] * cos

    t_row = lax.broadcasted_iota(jnp.int32, (c_len, RET_QK_DIM), 0).astype(F32)
    q_scale_f = jnp.exp(lg_f * (t_row + 1.0))
    q_scale_b = jnp.exp(lg_b * (c_len - t_row))
    k_scale_f = jnp.exp(lg_f * (c_len - 1.0 - t_row))
    k_scale_b = jnp.exp(lg_b * t_row)
    diff = (lax.broadcasted_iota(jnp.int32, (c_len, c_len), 0)
            - lax.broadcasted_iota(jnp.int32, (c_len, c_len), 1)).astype(F32)
    dmat = jnp.exp(jnp.where(diff >= 0, lg_f * diff, -lg_b * diff))
    decay_f = jnp.exp(jnp.full((1, RET_V_DIM), lg_f * c_len, F32))
    decay_b = jnp.exp(jnp.full((1, RET_V_DIM), lg_b * c_len, F32))

    sf_ref[...] = jnp.zeros_like(sf_ref)
    sb_ref[...] = jnp.zeros_like(sb_ref)

    def rows(c):
        return pl.ds(pl.multiple_of(c * c_len, c_len), c_len)

    def forward(c, carry):
        r = rows(c)
        qc = qs_ref[r, :]
        kc = ks_ref[r, :]
        vc = v_ref[r, :]
        scores = lax.dot_general(qc.astype(BF16), kc.astype(BF16), (((1,), (1,)), ((), ())),
                                 preferred_element_type=F32) * dmat
        y = jnp.dot(scores.astype(BF16), vc, preferred_element_type=F32)
        y = y + jnp.dot((qc * q_scale_f).astype(BF16), sf_ref[...].astype(BF16),
                        preferred_element_type=F32)
        ys_ref[r, :] = y
        kv = lax.dot_general((kc * k_scale_f).astype(BF16), vc, (((0,), (0,)), ((), ())),
                             preferred_element_type=F32)
        sf_ref[...] = decay_f * sf_ref[...] + kv
        return carry

    lax.fori_loop(0, n_chunks, forward, 0)

    def backward(i, carry):
        c = n_chunks - 1 - i
        r = rows(c)
        qc = qs_ref[r, :]
        kc = ks_ref[r, :]
        vc = v_ref[r, :]
        y = ys_ref[r, :] + jnp.dot((qc * q_scale_b).astype(BF16), sb_ref[...].astype(BF16),
                                   preferred_element_type=F32)
        kv = lax.dot_general((kc * k_scale_b).astype(BF16), vc, (((0,), (0,)), ((), ())),
                             preferred_element_type=F32)
        sb_ref[...] = decay_b * sb_ref[...] + kv
        yn = y * lax.rsqrt(jnp.mean(y * y, axis=-1, keepdims=True) + EPS)
        g = g_ref[r, :].astype(F32)
        gr = gr_ref[r, :].astype(F32)
        o_ref[r, :] = (jax.nn.sigmoid(gr) * (g * jax.nn.sigmoid(g) * yn)).astype(o_ref.dtype)
        return carry

    lax.fori_loop(0, n_chunks, backward, 0)


def _retention(proj3, log_gamma, cos, sin):
    qk_blk = RET_QK_DIM
    v_blk = RET_V_DIM
    return pl.pallas_call(
        _retention_kernel,
        grid=(BATCH, RET_HEADS),
        in_specs=[
            pl.BlockSpec(memory_space=pltpu.SMEM),
            pl.BlockSpec((None, SEQ, qk_blk), lambda b, h: (b, 0, COL_Q // qk_blk + h)),
            pl.BlockSpec((None, SEQ, qk_blk), lambda b, h: (b, 0, COL_K // qk_blk + h)),
            pl.BlockSpec((None, SEQ, v_blk), lambda b, h: (b, 0, COL_V // v_blk + h)),
            pl.BlockSpec((None, SEQ, v_blk), lambda b, h: (b, 0, COL_G // v_blk + h)),
            pl.BlockSpec((None, SEQ, v_blk), lambda b, h: (b, 0, COL_GATE_R // v_blk + h)),
            pl.BlockSpec((SEQ, RET_QK_DIM // 2), lambda b, h: (0, 0)),
            pl.BlockSpec((SEQ, RET_QK_DIM // 2), lambda b, h: (0, 0)),
        ],
        out_specs=pl.BlockSpec((None, SEQ, v_blk), lambda b, h: (b, 0, h)),
        out_shape=jax.ShapeDtypeStruct((BATCH, SEQ, D_MODEL), BF16),
        scratch_shapes=[pltpu.VMEM((SEQ, RET_QK_DIM), F32),
                        pltpu.VMEM((SEQ, RET_QK_DIM), F32),
                        pltpu.VMEM((SEQ, RET_V_DIM), F32),
                        pltpu.VMEM((RET_QK_DIM, RET_V_DIM), F32),
                        pltpu.VMEM((RET_QK_DIM, RET_V_DIM), F32)],
        compiler_params=_params(("parallel", "arbitrary")),
        name="retention",
    )(log_gamma, proj3, proj3, proj3, proj3, proj3, cos, sin)


def _ssm_operators(a_re, a_im, log_dt, b_re, b_im, c_re, c_im):
    t_len, p = SSM_T, SSM_STATE
    n_state = (np.arange(t_len - 1, -1, -1, dtype=np.float32), np.arange(t_len, dtype=np.float32))
    n_out = (np.arange(1, t_len + 1, dtype=np.float32), np.arange(t_len, 0, -1, dtype=np.float32))
    state_parts, inter_parts, coeff_rows, s_cat, c_tile = [], [], [], [], []
    for d in range(2):
        dt = jnp.exp(log_dt[d])[:, None]
        lr, li = a_re[d], a_im[d]
        lrdt, lidt = lr * dt, li * dt

        def powers(n):
            mag = jnp.exp(n[None, :, None] * lrdt[:, None, :])
            ang = n[None, :, None] * lidt[:, None, :]
            return mag * jnp.cos(ang), mag * jnp.sin(ang)

        (lbr, lbi) = [v[:, 0] for v in powers(np.ones((1,), np.float32))]
        den = lr * lr + li * li
        nr, ni = lbr - 1.0, lbi
        fr, fi = (nr * lr + ni * li) / den, (ni * lr - nr * li) / den
        bt_r = jnp.swapaxes(b_re[d], 1, 2)
        bt_i = jnp.swapaxes(b_im[d], 1, 2)
        bbr = fr[:, None, :] * bt_r - fi[:, None, :] * bt_i
        bbi = fr[:, None, :] * bt_i + fi[:, None, :] * bt_r

        pr, pi = powers(n_state[d])
        s_r = (pr[:, :, None, :] * bbr[:, None] - pi[:, :, None, :] * bbi[:, None]
               ).reshape(SSM_GROUPS, SSM_LANES, p)
        s_i = (pr[:, :, None, :] * bbi[:, None] + pi[:, :, None, :] * bbr[:, None]
               ).reshape(SSM_GROUPS, SSM_LANES, p)
        state_parts += [s_r, s_i, s_i, s_r]
        s_cat.append(jnp.concatenate([s_r, s_i], axis=-1))

        cr, ci = c_re[d], c_im[d]
        qr, qi = powers(n_out[d])
        i_r = (cr[:, None] * qr[:, :, None, :] - ci[:, None] * qi[:, :, None, :]
               ).reshape(SSM_GROUPS, SSM_LANES, p)
        i_i = (cr[:, None] * qi[:, :, None, :] + ci[:, None] * qr[:, :, None, :]
               ).reshape(SSM_GROUPS, SSM_LANES, p)
        inter_parts += [i_r, -i_i]
        cc = jnp.concatenate([cr, -ci], axis=-1)
        c_tile.append(jnp.broadcast_to(cc[:, None], (SSM_GROUPS, t_len, SSM_GROUP, 2 * p)
                                       ).reshape(SSM_GROUPS, SSM_LANES, 2 * p))

        ar, ai = [v[:, 0] for v in powers(np.full((1,), t_len, np.float32))]
        a1 = jnp.concatenate([ar, ar], axis=-1)
        a2 = jnp.concatenate([-ai, ai], axis=-1)
        coeff_rows += [a1, a2, -a2]

    w_state = jnp.concatenate(state_parts, axis=-1).astype(BF16)
    w_inter_t = jnp.concatenate(inter_parts, axis=-1).astype(BF16)
    coeffs = jnp.stack(coeff_rows, axis=1)
    coeffs = jnp.broadcast_to(coeffs[:, :, None, :], (SSM_GROUPS, 6, BATCH, 2 * p)).astype(F32)
    return w_state, w_inter_t, coeffs, s_cat, c_tile


def _ssm_intra_kernel(sf_ref, cf_ref, sb_ref, cb_ref, o_ref):
    nt = (((1,), (1,)), ((), ()))
    blk = lax.broadcasted_iota(jnp.int32, (SSM_LANES, SSM_LANES), 1) // SSM_GROUP
    for g in range(o_ref.shape[0]):
        kf = lax.dot_general(sf_ref[g], cf_ref[g], nt, precision=lax.Precision.HIGHEST,
                             preferred_element_type=F32)
        kb = lax.dot_general(sb_ref[g], cb_ref[g], nt, precision=lax.Precision.HIGHEST,
                             preferred_element_type=F32)
        acc = jnp.zeros((SSM_LANES, SSM_LANES), F32)
        for t in range(SSM_T):
            up = (SSM_T - 1 - t) * SSM_GROUP
            down = t * SSM_GROUP
            parts_f = [kf[up:, :]] + ([jnp.zeros((up, SSM_LANES), F32)] if up else [])
            parts_b = ([jnp.zeros((down, SSM_LANES), F32)] if down else []) + [kb[:SSM_LANES - down, :]]
            shifted = jnp.concatenate(parts_f, axis=0) + jnp.concatenate(parts_b, axis=0)
            acc = jnp.where(blk == t, shifted, acc)
        o_ref[g] = acc.astype(o_ref.dtype)


def _ssm_intra(s_cat, c_tile):
    gb = 8
    spec = pl.BlockSpec((gb, SSM_LANES, 2 * SSM_STATE), lambda i: (i, 0, 0))
    return pl.pallas_call(
        _ssm_intra_kernel,
        grid=(SSM_GROUPS // gb,),
        in_specs=[spec, spec, spec, spec],
        out_specs=pl.BlockSpec((gb, SSM_LANES, SSM_LANES), lambda i: (i, 0, 0)),
        out_shape=jax.ShapeDtypeStruct((SSM_GROUPS, SSM_LANES, SSM_LANES), BF16),
        compiler_params=_params(("parallel",)),
        name="ssm_intra_weights",
    )(s_cat[0], c_tile[0], s_cat[1], c_tile[1])


def _lane_block_ids():
    return lax.broadcasted_iota(jnp.int32, (BATCH, LANES), 1) // SSM_GROUP


def _token_rows(c, t):
    return pl.ds(pl.multiple_of((c * SSM_T + t) * BATCH, BATCH), BATCH)


def _block_transpose(src, blk):
    n = BLOCKS_PER_VREG
    rolled = []
    for d in range(n):
        m = src[d % n]
        for a in range(1, n):
            m = jnp.where(blk == a, src[(a + d) % n], m)
        rolled.append(pltpu.roll(m, d * SSM_GROUP, axis=1) if d else m)
    out = []
    for a in range(n):
        t = rolled[(0 - a) % n]
        for b in range(1, n):
            t = jnp.where(blk == b, rolled[(b - a) % n], t)
        out.append(t)
    return out


def _ssm_pack_kernel(u_ref, o_ref):
    blk = _lane_block_ids()

    def body(it, carry):
        for pair in range(PACK_CHUNKS // 2):
            halves = []
            for k in range(2):
                c = it * PACK_CHUNKS + 2 * pair + k
                parts = []
                for s_hi in range(SSM_T // BLOCKS_PER_VREG):
                    v = [u_ref[_token_rows(c, s_hi * BLOCKS_PER_VREG + s_lo), :]
                         for s_lo in range(BLOCKS_PER_VREG)]
                    parts.append(_block_transpose(v, blk))
                halves.append([jnp.concatenate([p[gl] for p in parts], axis=1)
                               for gl in range(GROUPS_PER_VREG)])
            first = (it * (PACK_CHUNKS // 2) + pair) * 2 * BATCH
            rows = pl.ds(pl.multiple_of(first, 2 * BATCH), 2 * BATCH)
            for gl in range(GROUPS_PER_VREG):
                o_ref[gl, rows, :] = jnp.concatenate([halves[0][gl], halves[1][gl]],
                                                     axis=0).astype(BF16)
        return carry

    lax.fori_loop(0, SSM_CHUNKS // PACK_CHUNKS, body, 0)


def _ssm_pack(u_lb):
    return pl.pallas_call(
        _ssm_pack_kernel,
        grid=(D_MODEL // LANES,),
        in_specs=[pl.BlockSpec((None, TOKENS, LANES), lambda i: (i, 0, 0))],
        out_specs=pl.BlockSpec((GROUPS_PER_VREG, SSM_ROWS, SSM_LANES), lambda i: (i, 0, 0)),
        out_shape=jax.ShapeDtypeStruct((SSM_GROUPS, SSM_ROWS, SSM_LANES), BF16),
        compiler_params=_params(("parallel",)),
        name="ssm_pack",
    )(u_lb)


def _gelu_tanh(x):
    return 0.5 * x * (1.0 + jnp.tanh(math.sqrt(2.0 / math.pi) * (x + 0.044715 * (x * x * x))))


def _ssm_unpack_kernel(y_ref, u_ref, d_ref, o_ref):
    blk = _lane_block_ids()
    d = jnp.broadcast_to(d_ref[...], (BATCH, LANES))

    def body(it, carry):
        for pair in range(PACK_CHUNKS // 2):
            first = (it * (PACK_CHUNKS // 2) + pair) * 2 * BATCH
            rows = pl.ds(pl.multiple_of(first, 2 * BATCH), 2 * BATCH)
            y = [y_ref[gl, rows, :].astype(F32) for gl in range(GROUPS_PER_VREG)]
            for k in range(2):
                c = it * PACK_CHUNKS + 2 * pair + k
                for t_hi in range(SSM_T // BLOCKS_PER_VREG):
                    src = [y[gl][k * BATCH:(k + 1) * BATCH, t_hi * LANES:(t_hi + 1) * LANES]
                           for gl in range(GROUPS_PER_VREG)]
                    out = _block_transpose(src, blk)
                    for t_lo in range(BLOCKS_PER_VREG):
                        tok = _token_rows(c, t_hi * BLOCKS_PER_VREG + t_lo)
                        o_ref[tok, :] = _gelu_tanh(out[t_lo] + d * u_ref[tok, :])
        return carry

    lax.fori_loop(0, SSM_CHUNKS // PACK_CHUNKS, body, 0)


def _ssm_unpack(y_t, u_lb, d_skip):
    slab = pl.BlockSpec((None, TOKENS, LANES), lambda i: (i, 0, 0))
    return pl.pallas_call(
        _ssm_unpack_kernel,
        grid=(D_MODEL // LANES,),
        in_specs=[pl.BlockSpec((GROUPS_PER_VREG, SSM_ROWS, SSM_LANES), lambda i: (i, 0, 0)),
                  slab,
                  pl.BlockSpec((1, LANES), lambda i: (0, i))],
        out_specs=slab,
        out_shape=jax.ShapeDtypeStruct((D_MODEL // LANES, TOKENS, LANES), F32),
        compiler_params=_params(("parallel",)),
        name="ssm_unpack_gelu",
    )(y_t, u_lb, d_skip)


def _ssm_kernel(u_ref, wi_ref, ws_ref, wx_ref, a_ref, o_ref, s_ref, x_ref):
    p2 = 2 * SSM_STATE
    nt = (((1,), (1,)), ((), ()))
    for g in range(SSM_GB):
        s_ref[g] = jnp.dot(u_ref[g], ws_ref[g], preferred_element_type=F32)

    def step(c, carry):
        rf = pl.ds(pl.multiple_of(c * BATCH, BATCH), BATCH)
        rb = pl.ds(pl.multiple_of((SSM_CHUNKS - 1 - c) * BATCH, BATCH), BATCH)
        new = []
        for g in range(SSM_GB):
            xf, xfs, xb, xbs = carry[4 * g:4 * g + 4]
            x_ref[g, rf, 0:p2] = xf
            x_ref[g, rb, p2:2 * p2] = xb
            nxf = a_ref[g, 0] * xf + a_ref[g, 1] * xfs + s_ref[g, rf, 0:p2]
            nxfs = a_ref[g, 0] * xfs + a_ref[g, 2] * xf + s_ref[g, rf, p2:2 * p2]
            nxb = a_ref[g, 3] * xb + a_ref[g, 4] * xbs + s_ref[g, rb, 2 * p2:3 * p2]
            nxbs = a_ref[g, 3] * xbs + a_ref[g, 5] * xb + s_ref[g, rb, 3 * p2:4 * p2]
            new += [nxf, nxfs, nxb, nxbs]
        return tuple(new)

    zero = jnp.zeros((BATCH, p2), F32)
    lax.fori_loop(0, SSM_CHUNKS, step, (zero,) * (4 * SSM_GB))

    for g in range(SSM_GB):
        y = jnp.dot(u_ref[g], wi_ref[g], preferred_element_type=F32)
        y = y + lax.dot_general(x_ref[g].astype(BF16), wx_ref[g], nt, preferred_element_type=F32)
        o_ref[g] = y.astype(o_ref.dtype)


def _ssm(u_t, w_intra, w_state, w_inter_t, coeffs):
    gb = SSM_GB
    wspec = pl.BlockSpec((gb, SSM_LANES, SSM_LANES), lambda i: (i, 0, 0))
    return pl.pallas_call(
        _ssm_kernel,
        grid=(SSM_GROUPS // gb,),
        in_specs=[pl.BlockSpec((gb, SSM_ROWS, SSM_LANES), lambda i: (i, 0, 0)),
                  wspec,
                  pl.BlockSpec((gb, SSM_LANES, 8 * SSM_STATE), lambda i: (i, 0, 0)),
                  wspec,
                  pl.BlockSpec((gb, 6, BATCH, 2 * SSM_STATE), lambda i: (i, 0, 0, 0))],
        out_specs=pl.BlockSpec((gb, SSM_ROWS, SSM_LANES), lambda i: (i, 0, 0)),
        out_shape=jax.ShapeDtypeStruct((SSM_GROUPS, SSM_ROWS, SSM_LANES), BF16),
        scratch_shapes=[pltpu.VMEM((gb, SSM_ROWS, 8 * SSM_STATE), F32),
                        pltpu.VMEM((gb, SSM_ROWS, 4 * SSM_STATE), F32)],
        compiler_params=_params(("parallel",)),
        name="ssm_core",
    )(u_t, w_intra, w_state, w_inter_t, coeffs)


def _glu_merge_kernel(a_ref, w_ref, b_ref, gs_ref, ret_ref, o_ref, af_ref, ab_ref):
    rows = BATCH * GLU_TL
    for k in range(D_MODEL // LANES):
        for b in range(BATCH):
            af_ref[b * GLU_TL:(b + 1) * GLU_TL, k * LANES:(k + 1) * LANES] = (
                a_ref[k, pl.ds(b, GLU_TL, stride=BATCH), :])
    ab_ref[...] = af_ref[...].astype(BF16)
    for n in range(D_MODEL // GLU_TN):
        cols = slice(n * GLU_TN, (n + 1) * GLU_TN)
        z = jnp.dot(ab_ref[...], w_ref[:, cols], preferred_element_type=F32) + b_ref[:, cols]
        ssm_out = af_ref[:, cols] * jax.nn.sigmoid(z)
        gate = jax.nn.sigmoid(gs_ref[:, :, cols].reshape(rows, GLU_TN).astype(F32))
        merged = ret_ref[:, :, cols].reshape(rows, GLU_TN).astype(F32) + gate * ssm_out
        o_ref[:, :, cols] = merged.astype(o_ref.dtype).reshape(BATCH, GLU_TL, GLU_TN)


def _glu_merge(a_lb, proj3, w_glu, b_glu, ret_g3):
    rows = BATCH * GLU_TL
    tile = pl.BlockSpec((BATCH, GLU_TL, D_MODEL), lambda i: (0, i, 0))
    return pl.pallas_call(
        _glu_merge_kernel,
        grid=(SEQ // GLU_TL,),
        in_specs=[pl.BlockSpec((D_MODEL // LANES, rows, LANES), lambda i: (0, i, 0)),
                  pl.BlockSpec((D_MODEL, D_MODEL), lambda i: (0, 0), pipeline_mode=pl.Buffered(1)),
                  pl.BlockSpec((1, D_MODEL), lambda i: (0, 0)),
                  pl.BlockSpec((BATCH, GLU_TL, D_MODEL), lambda i: (0, i, COL_GATE_S // D_MODEL)),
                  tile],
        out_specs=tile,
        out_shape=jax.ShapeDtypeStruct((BATCH, SEQ, D_MODEL), BF16),
        scratch_shapes=[pltpu.VMEM((rows, D_MODEL), F32), pltpu.VMEM((rows, D_MODEL), BF16)],
        compiler_params=_params(("parallel",)),
        name="glu_merge",
    )(a_lb, w_glu, b_glu, proj3, ret_g3)


def _proj_residual_kernel(a_ref, w_ref, x_ref, o_ref):
    o_ref[...] = x_ref[...] + jnp.dot(a_ref[...], w_ref[...], preferred_element_type=F32)


def _proj_residual(a, w, x):
    m, kdim = a.shape
    n = w.shape[1]
    return pl.pallas_call(
        _proj_residual_kernel,
        grid=(m // TM, n // TN),
        in_specs=[pl.BlockSpec((TM, kdim), lambda i, j: (i, 0)),
                  pl.BlockSpec((kdim, TN), lambda i, j: (0, j)),
                  pl.BlockSpec((TM, TN), lambda i, j: (i, j))],
        out_specs=pl.BlockSpec((TM, TN), lambda i, j: (i, j)),
        out_shape=jax.ShapeDtypeStruct((m, n), F32),
        compiler_params=_params(("parallel", "parallel")),
        name="out_proj_residual",
    )(a, w, x)


def _ffn_kernel(x_ref, g_ref, wg_ref, wu_ref, wd_ref, gf_ref, o_ref, h_ref, acc_ref, *, final_norm):
    j = pl.program_id(1)

    @pl.when(j == 0)
    def _():
        h_ref[...] = _rms(x_ref[...], g_ref[...]).astype(BF16)
        acc_ref[...] = jnp.zeros_like(acc_ref)

    h = h_ref[...]
    gate = jnp.dot(h, wg_ref[...], preferred_element_type=F32)
    up = jnp.dot(h, wu_ref[...], preferred_element_type=F32)
    act = (gate * jax.nn.sigmoid(gate) * up).astype(BF16)
    acc_ref[...] += jnp.dot(act, wd_ref[...], preferred_element_type=F32)

    @pl.when(j == pl.num_programs(1) - 1)
    def _():
        y = x_ref[...] + acc_ref[...]
        if final_norm:
            y = _rms(y, gf_ref[...])
        o_ref[...] = y


def _ffn(x, gain, w_gate, w_up, w_down, final_gain, final_norm):
    m, d = x.shape
    f = w_gate.shape[1]
    return pl.pallas_call(
        functools.partial(_ffn_kernel, final_norm=final_norm),
        grid=(m // FFN_TM, f // FFN_TF),
        in_specs=[pl.BlockSpec((FFN_TM, d), lambda i, j: (i, 0)),
                  pl.BlockSpec((1, d), lambda i, j: (0, 0)),
                  pl.BlockSpec((d, FFN_TF), lambda i, j: (0, j)),
                  pl.BlockSpec((d, FFN_TF), lambda i, j: (0, j)),
                  pl.BlockSpec((FFN_TF, d), lambda i, j: (j, 0)),
                  pl.BlockSpec((1, d), lambda i, j: (0, 0))],
        out_specs=pl.BlockSpec((FFN_TM, d), lambda i, j: (i, 0)),
        out_shape=jax.ShapeDtypeStruct((m, d), F32),
        scratch_shapes=[pltpu.VMEM((FFN_TM, d), BF16), pltpu.VMEM((FFN_TM, d), F32)],
        compiler_params=_params(("parallel", "arbitrary")),
        name="ffn",
    )(x, gain, w_gate, w_up, w_down, final_gain)


def _rotary_tables():
    half = RET_QK_DIM // 2
    inv = 1.0 / (ROPE_BASE ** (jnp.arange(half, dtype=F32) / half))
    ang = jnp.arange(SEQ, dtype=F32)[:, None] * inv[None, :]
    return jnp.cos(ang), jnp.sin(ang)


def kernel(x, ln_mix_g, w_in, ret_log_gamma, ssm_a_re, ssm_a_im, ssm_log_dt, ssm_b_re, ssm_b_im,
           ssm_c_re, ssm_c_im, ssm_d, w_glu, b_glu, w_out, ln_ffn_g, w_ffn_gate, w_ffn_up,
           w_ffn_down, ln_final_g):
    depth = w_in.shape[0]
    cos, sin = _rotary_tables()
    xt = x.reshape(TOKENS, D_MODEL)
    final_gain = ln_final_g.reshape(1, D_MODEL)
    for i in range(depth):
        proj3, u_lb = _norm_proj(xt.reshape(BATCH, SEQ, D_MODEL), ln_mix_g[i].reshape(1, D_MODEL),
                                 w_in[i].astype(BF16))
        ret_g3 = _retention(proj3, ret_log_gamma[i].astype(F32), cos, sin)
        w_state, w_inter_t, coeffs, s_cat, c_tile = _ssm_operators(
            ssm_a_re[i], ssm_a_im[i], ssm_log_dt[i], ssm_b_re[i], ssm_b_im[i], ssm_c_re[i], ssm_c_im[i])
        w_intra = _ssm_intra(s_cat, c_tile)
        y_t = _ssm(_ssm_pack(u_lb), w_intra, w_state, w_inter_t, coeffs)
        a_lb = _ssm_unpack(y_t, u_lb, ssm_d[i].reshape(1, D_MODEL))
        merged3 = _glu_merge(a_lb, proj3, w_glu[i].astype(BF16), b_glu[i].reshape(1, D_MODEL), ret_g3)
        xt = _proj_residual(merged3.reshape(TOKENS, D_MODEL), w_out[i].astype(BF16), xt)
        xt = _ffn(xt, ln_ffn_g[i].reshape(1, D_MODEL), w_ffn_gate[i].astype(BF16),
                  w_ffn_up[i].astype(BF16), w_ffn_down[i].astype(BF16), final_gain,
                  final_norm=(i == depth - 1))
    return xt.reshape(BATCH, SEQ, D_MODEL)
```

```python
import functools
import math

import numpy as np
import jax
import jax.numpy as jnp
from jax import lax
from jax.experimental import pallas as pl
from jax.experimental.pallas import tpu as pltpu

F32 = jnp.float32
BF16 = jnp.bfloat16

D_MODEL = 2048
BATCH = 8
SEQ = 2048
TOKENS = BATCH * SEQ
RET_HEADS = 4
RET_QK_DIM = 256
RET_V_DIM = D_MODEL // RET_HEADS
RET_QK_WIDTH = RET_HEADS * RET_QK_DIM
ROPE_BASE = 10000.0
SSM_GROUP = 16
SSM_GROUPS = D_MODEL // SSM_GROUP
SSM_STATE = 64
D_FF = ((8 * D_MODEL // 3 + 255) // 256) * 256
IN_WIDTH = 2 * RET_QK_WIDTH + 5 * D_MODEL
EPS = 1e-6

COL_Q = 0
COL_K = RET_QK_WIDTH
COL_V = 2 * RET_QK_WIDTH
COL_G = COL_V + D_MODEL
COL_U = COL_G + D_MODEL
COL_GATE_R = COL_U + D_MODEL
COL_GATE_S = COL_GATE_R + D_MODEL

LANES = 128
RET_CHUNK = 256
SSM_T = 16
SSM_CHUNKS = SEQ // SSM_T
SSM_ROWS = SSM_CHUNKS * BATCH
SSM_LANES = SSM_T * SSM_GROUP
SSM_GB = 4
SSM_WGB = 8
GROUPS_PER_VREG = LANES // SSM_GROUP
BLOCKS_PER_VREG = LANES // SSM_GROUP

VMEM_LIMIT = 56 * 1024 * 1024

TM = 1024
PROJ_TL = 128
TN = 1024
GLU_TL = 64
GLU_TN = 512
PACK_CHUNKS = 4
FFN_TM = 512
FFN_TF = 512


def _params(sem):
    return pltpu.CompilerParams(dimension_semantics=sem, vmem_limit_bytes=VMEM_LIMIT)


def _rms(x, g):
    return x * lax.rsqrt(jnp.mean(x * x, axis=-1, keepdims=True) + EPS) * g


def _sigmoid(x):
    return 0.5 * jnp.tanh(0.5 * x) + 0.5


def _silu(x):
    h = 0.5 * x
    return h + h * jnp.tanh(h)


def _gelu_tanh(x):
    return 0.5 * x * (1.0 + jnp.tanh(math.sqrt(2.0 / math.pi) * (x + 0.044715 * (x * x * x))))


def _norm_proj_kernel(x_ref, g_ref, w_ref, o_ref, u_ref, h_ref):
    j = pl.program_id(1)
    rows = BATCH * PROJ_TL

    @pl.when(j == 0)
    def _():
        h_ref[...] = _rms(x_ref[...].reshape(rows, D_MODEL), g_ref[...]).astype(BF16)

    acc = jnp.dot(h_ref[...], w_ref[...], preferred_element_type=F32)
    o_ref[...] = acc.astype(o_ref.dtype).reshape(BATCH, PROJ_TL, TN)

    @pl.when((j >= COL_U // TN) & (j < COL_GATE_R // TN))
    def _():
        for k in range(TN // LANES):
            for b in range(BATCH):
                u_ref[k, pl.ds(b, PROJ_TL, stride=BATCH), :] = (
                    acc[b * PROJ_TL:(b + 1) * PROJ_TL, k * LANES:(k + 1) * LANES])


def _norm_proj(x3, gain, w, layer):
    n = w.shape[-1]
    u_lo, u_hi = COL_U // TN, COL_GATE_R // TN - 1
    rows = BATCH * PROJ_TL
    return pl.pallas_call(
        _norm_proj_kernel,
        grid=(SEQ // PROJ_TL, n // TN),
        in_specs=[pl.BlockSpec((BATCH, PROJ_TL, D_MODEL), lambda i, j: (0, i, 0)),
                  pl.BlockSpec((None, 1, D_MODEL), lambda i, j: (layer, 0, 0)),
                  pl.BlockSpec((None, D_MODEL, TN), lambda i, j: (layer, 0, j))],
        out_specs=[pl.BlockSpec((BATCH, PROJ_TL, TN), lambda i, j: (0, i, j)),
                   pl.BlockSpec((TN // LANES, rows, LANES),
                                lambda i, j: (jnp.clip(j, u_lo, u_hi) - u_lo, i, 0))],
        out_shape=[jax.ShapeDtypeStruct((BATCH, SEQ, n), BF16),
                   jax.ShapeDtypeStruct((D_MODEL // LANES, TOKENS, LANES), F32)],
        scratch_shapes=[pltpu.VMEM((rows, D_MODEL), BF16)],
        compiler_params=_params(("parallel", "arbitrary")),
        name="norm_in_proj",
    )(x3, gain, w)


def _retention_kernel(lg_ref, q_ref, k_ref, v_ref, g_ref, gr_ref, cos_ref, sin_ref, o_ref,
                      qs_ref, ks_ref, ys_ref, sf_ref, sb_ref, *, layer):
    c_len = RET_CHUNK
    n_chunks = SEQ // c_len
    half = RET_QK_DIM // 2
    head = pl.program_id(1)
    lg_f = lg_ref[layer, 0, head]
    lg_b = lg_ref[layer, 1, head]

    cos = cos_ref[...]
    sin = sin_ref[...]
    q = q_ref[...].astype(F32)
    k = k_ref[...].astype(F32) * (RET_QK_DIM ** -0.5)
    qs_ref[:, :half] = q[:, :half] * cos - q[:, half:] * sin
    qs_ref[:, half:] = q[:, :half] * sin + q[:, half:] * cos
    ks_ref[:, :half] = k[:, :half] * cos - k[:, half:] * sin
    ks_ref[:, half:] = k[:, :half] * sin + k[:, half:] * cos

    t_row = lax.broadcasted_iota(jnp.int32, (c_len, RET_QK_DIM), 0).astype(F32)
    q_scale_f = jnp.exp(lg_f * (t_row + 1.0))
    q_scale_b = jnp.exp(lg_b * (c_len - t_row))
    k_scale_f = jnp.exp(lg_f * (c_len - 1.0 - t_row))
    k_scale_b = jnp.exp(lg_b * t_row)
    diff = (lax.broadcasted_iota(jnp.int32, (c_len, c_len), 0)
            - lax.broadcasted_iota(jnp.int32, (c_len, c_len), 1)).astype(F32)
    dmat = jnp.exp(jnp.where(diff >= 0, lg_f * diff, -lg_b * diff))
    decay_f = jnp.exp(jnp.full((1, RET_V_DIM), lg_f * c_len, F32))
    decay_b = jnp.exp(jnp.full((1, RET_V_DIM), lg_b * c_len, F32))

    sf_ref[...] = jnp.zeros_like(sf_ref)
    sb_ref[...] = jnp.zeros_like(sb_ref)

    def rows(c):
        return pl.ds(pl.multiple_of(c * c_len, c_len), c_len)

    def forward(c, carry):
        r = rows(c)
        qc = qs_ref[r, :]
        kc = ks_ref[r, :]
        vc = v_ref[r, :]
        scores = lax.dot_general(qc.astype(BF16), kc.astype(BF16), (((1,), (1,)), ((), ())),
                                 preferred_element_type=F32) * dmat
        y = jnp.dot(scores.astype(BF16), vc, preferred_element_type=F32)
        y = y + jnp.dot((qc * q_scale_f).astype(BF16), sf_ref[...].astype(BF16),
                        preferred_element_type=F32)
        ys_ref[r, :] = y
        kv = lax.dot_general((kc * k_scale_f).astype(BF16), vc, (((0,), (0,)), ((), ())),
                             preferred_element_type=F32)
        sf_ref[...] = decay_f * sf_ref[...] + kv
        return carry

    lax.fori_loop(0, n_chunks, forward, 0)

    def backward(i, carry):
        c = n_chunks - 1 - i
        r = rows(c)
        qc = qs_ref[r, :]
        kc = ks_ref[r, :]
        vc = v_ref[r, :]
        y = ys_ref[r, :] + jnp.dot((qc * q_scale_b).astype(BF16), sb_ref[...].astype(BF16),
                                   preferred_element_type=F32)
        kv = lax.dot_general((kc * k_scale_b).astype(BF16), vc, (((0,), (0,)), ((), ())),
                             preferred_element_type=F32)
        sb_ref[...] = decay_b * sb_ref[...] + kv
        yn = y * lax.rsqrt(jnp.mean(y * y, axis=-1, keepdims=True) + EPS)
        g = g_ref[r, :].astype(F32)
        gr = gr_ref[r, :].astype(F32)
        o_ref[r, :] = (_sigmoid(gr) * (_silu(g) * yn)).astype(o_ref.dtype)
        return carry

    lax.fori_loop(0, n_chunks, backward, 0)


def _retention(proj3, log_gamma, cos, sin, layer):
    qk_blk = RET_QK_DIM
    v_blk = RET_V_DIM
    return pl.pallas_call(
        functools.partial(_retention_kernel, layer=layer),
        grid=(BATCH, RET_HEADS),
        in_specs=[
            pl.BlockSpec(memory_space=pltpu.SMEM),
            pl.BlockSpec((None, SEQ, qk_blk), lambda b, h: (b, 0, COL_Q // qk_blk + h)),
            pl.BlockSpec((None, SEQ, qk_blk), lambda b, h: (b, 0, COL_K // qk_blk + h)),
            pl.BlockSpec((None, SEQ, v_blk), lambda b, h: (b, 0, COL_V // v_blk + h)),
            pl.BlockSpec((None, SEQ, v_blk), lambda b, h: (b, 0, COL_G // v_blk + h)),
            pl.BlockSpec((None, SEQ, v_blk), lambda b, h: (b, 0, COL_GATE_R // v_blk + h)),
            pl.BlockSpec((SEQ, RET_QK_DIM // 2), lambda b, h: (0, 0)),
            pl.BlockSpec((SEQ, RET_QK_DIM // 2), lambda b, h: (0, 0)),
        ],
        out_specs=pl.BlockSpec((None, SEQ, v_blk), lambda b, h: (b, 0, h)),
        out_shape=jax.ShapeDtypeStruct((BATCH, SEQ, D_MODEL), BF16),
        scratch_shapes=[pltpu.VMEM((SEQ, RET_QK_DIM), F32),
                        pltpu.VMEM((SEQ, RET_QK_DIM), F32),
                        pltpu.VMEM((SEQ, RET_V_DIM), F32),
                        pltpu.VMEM((RET_QK_DIM, RET_V_DIM), F32),
                        pltpu.VMEM((RET_QK_DIM, RET_V_DIM), F32)],
        compiler_params=_params(("parallel", "arbitrary")),
        name="retention",
    )(log_gamma, proj3, proj3, proj3, proj3, proj3, cos, sin)


def _ssm_factors(a_re, a_im, log_dt, b_re, b_im, c_re, c_im):
    t_len, p = SSM_T, SSM_STATE
    depth = a_re.shape[0]
    ar_t = np.arange(t_len, dtype=np.float32)
    n_state = np.stack([t_len - 1.0 - ar_t, ar_t])
    n_out = np.stack([ar_t + 1.0, t_len - ar_t])
    dt = jnp.exp(log_dt)[..., None]
    lr, li = a_re, a_im
    lrdt, lidt = lr * dt, li * dt

    def powers(n):
        e = n[None, :, None, :, None]
        mag = jnp.exp(e * lrdt[:, :, :, None, :])
        ang = e * lidt[:, :, :, None, :]
        return mag * jnp.cos(ang), mag * jnp.sin(ang)

    mag1 = jnp.exp(lrdt)
    lbr, lbi = mag1 * jnp.cos(lidt), mag1 * jnp.sin(lidt)
    den = lr * lr + li * li
    nr, ni = lbr - 1.0, lbi
    fr, fi = (nr * lr + ni * li) / den, (ni * lr - nr * li) / den
    bt_r = jnp.swapaxes(b_re, -1, -2)
    bt_i = jnp.swapaxes(b_im, -1, -2)
    bbr = fr[..., None, :] * bt_r - fi[..., None, :] * bt_i
    bbi = fr[..., None, :] * bt_i + fi[..., None, :] * bt_r

    shape5 = (depth, 2, SSM_GROUPS, SSM_LANES, p)
    pr, pi = powers(n_state)
    s_r = (pr[..., :, None, :] * bbr[..., None, :, :] - pi[..., :, None, :] * bbi[..., None, :, :]
           ).reshape(shape5)
    s_i = (pr[..., :, None, :] * bbi[..., None, :, :] + pi[..., :, None, :] * bbr[..., None, :, :]
           ).reshape(shape5)
    s_cat = jnp.concatenate([s_r, s_i], axis=-1)

    qr, qi = powers(n_out)
    cr, ci = c_re[..., None, :, :], c_im[..., None, :, :]
    i_r = (cr * qr[..., :, None, :] - ci * qi[..., :, None, :]).reshape(shape5)
    i_i = (cr * qi[..., :, None, :] + ci * qr[..., :, None, :]).reshape(shape5)
    i_cat = jnp.concatenate([i_r, -i_i], axis=-1)

    cc = jnp.concatenate([c_re, -c_im], axis=-1)
    c_tile = jnp.broadcast_to(cc[..., None, :, :], (depth, 2, SSM_GROUPS, t_len, SSM_GROUP, 2 * p)
                              ).reshape(depth, 2, SSM_GROUPS, SSM_LANES, 2 * p)

    mag_t = jnp.exp(t_len * lrdt)
    ar, ai = mag_t * jnp.cos(t_len * lidt), mag_t * jnp.sin(t_len * lidt)
    a1 = jnp.concatenate([ar, ar], axis=-1)
    a2 = jnp.concatenate([-ai, ai], axis=-1)
    coeffs = jnp.stack([a1, a2, -a2], axis=3)
    coeffs = coeffs.transpose(0, 2, 1, 3, 4).reshape(depth, SSM_GROUPS, 6, 2 * p)
    coeffs = jnp.broadcast_to(coeffs[:, :, :, None, :], (depth, SSM_GROUPS, 6, BATCH, 2 * p))
    return s_cat, i_cat, c_tile, coeffs.astype(F32)


def _ssm_weights_kernel(sf_ref, sb_ref, if_ref, ib_ref, cf_ref, cb_ref, wi_ref, ws_ref, wx_ref):
    nt = (((1,), (1,)), ((), ()))
    blk = lax.broadcasted_iota(jnp.int32, (SSM_LANES, SSM_LANES), 1) // SSM_GROUP
    for g in range(SSM_WGB):
        sf, sb = sf_ref[g], sb_ref[g]
        kf = lax.dot_general(sf, cf_ref[g], nt, precision=lax.Precision.HIGHEST,
                             preferred_element_type=F32)
        kb = lax.dot_general(sb, cb_ref[g], nt, precision=lax.Precision.HIGHEST,
                             preferred_element_type=F32)
        acc = jnp.zeros((SSM_LANES, SSM_LANES), F32)
        for t in range(SSM_T):
            up = (SSM_T - 1 - t) * SSM_GROUP
            down = t * SSM_GROUP
            parts_f = [kf[up:, :]] + ([jnp.zeros((up, SSM_LANES), F32)] if up else [])
            parts_b = ([jnp.zeros((down, SSM_LANES), F32)] if down else []) + [kb[:SSM_LANES - down, :]]
            shifted = jnp.concatenate(parts_f, axis=0) + jnp.concatenate(parts_b, axis=0)
            acc = jnp.where(blk == t, shifted, acc)
        wi_ref[g] = acc.astype(wi_ref.dtype)
        ws_ref[g] = jnp.concatenate([sf, pltpu.roll(sf, SSM_STATE, axis=1),
                                     sb, pltpu.roll(sb, SSM_STATE, axis=1)], axis=1).astype(ws_ref.dtype)
        wx_ref[g] = jnp.concatenate([if_ref[g], ib_ref[g]], axis=1).astype(wx_ref.dtype)


def _ssm_weights(s_cat, i_cat, c_tile):
    depth = s_cat.shape[0]
    gb = SSM_WGB
    blk = (None, None, gb, SSM_LANES, 2 * SSM_STATE)
    fwd = pl.BlockSpec(blk, lambda l, i: (l, 0, i, 0, 0))
    bwd = pl.BlockSpec(blk, lambda l, i: (l, 1, i, 0, 0))

    def out(n):
        return (pl.BlockSpec((None, gb, SSM_LANES, n), lambda l, i: (l, i, 0, 0)),
                jax.ShapeDtypeStruct((depth, SSM_GROUPS, SSM_LANES, n), BF16))

    specs, shapes = zip(out(SSM_LANES), out(8 * SSM_STATE), out(4 * SSM_STATE))
    return pl.pallas_call(
        _ssm_weights_kernel,
        grid=(depth, SSM_GROUPS // gb),
        in_specs=[fwd, bwd, fwd, bwd, fwd, bwd],
        out_specs=list(specs),
        out_shape=list(shapes),
        compiler_params=_params(("parallel", "parallel")),
        name="ssm_weights",
    )(s_cat, s_cat, i_cat, i_cat, c_tile, c_tile)


def _lane_block_ids():
    return lax.broadcasted_iota(jnp.int32, (BATCH, LANES), 1) // SSM_GROUP


def _token_rows(c, t):
    return pl.ds(pl.multiple_of((c * SSM_T + t) * BATCH, BATCH), BATCH)


def _block_transpose(src, blk):
    n = BLOCKS_PER_VREG
    rolled = []
    for d in range(n):
        m = src[d % n]
        for a in range(1, n):
            m = jnp.where(blk == a, src[(a + d) % n], m)
        rolled.append(pltpu.roll(m, d * SSM_GROUP, axis=1) if d else m)
    out = []
    for a in range(n):
        t = rolled[(0 - a) % n]
        for b in range(1, n):
            t = jnp.where(blk == b, rolled[(b - a) % n], t)
        out.append(t)
    return out


def _ssm_pack_kernel(u_ref, o_ref):
    blk = _lane_block_ids()

    def body(it, carry):
        for pair in range(PACK_CHUNKS // 2):
            halves = []
            for k in range(2):
                c = it * PACK_CHUNKS + 2 * pair + k
                parts = []
                for s_hi in range(SSM_T // BLOCKS_PER_VREG):
                    v = [u_ref[_token_rows(c, s_hi * BLOCKS_PER_VREG + s_lo), :]
                         for s_lo in range(BLOCKS_PER_VREG)]
                    parts.append(_block_transpose(v, blk))
                halves.append([jnp.concatenate([p[gl] for p in parts], axis=1)
                               for gl in range(GROUPS_PER_VREG)])
            first = (it * (PACK_CHUNKS // 2) + pair) * 2 * BATCH
            rows = pl.ds(pl.multiple_of(first, 2 * BATCH), 2 * BATCH)
            for gl in range(GROUPS_PER_VREG):
                o_ref[gl, rows, :] = jnp.concatenate([halves[0][gl], halves[1][gl]],
                                                     axis=0).astype(BF16)
        return carry

    lax.fori_loop(0, SSM_CHUNKS // PACK_CHUNKS, body, 0)


def _ssm_pack(u_lb):
    return pl.pallas_call(
        _ssm_pack_kernel,
        grid=(D_MODEL // LANES,),
        in_specs=[pl.BlockSpec((None, TOKENS, LANES), lambda i: (i, 0, 0))],
        out_specs=pl.BlockSpec((GROUPS_PER_VREG, SSM_ROWS, SSM_LANES), lambda i: (i, 0, 0)),
        out_shape=jax.ShapeDtypeStruct((SSM_GROUPS, SSM_ROWS, SSM_LANES), BF16),
        compiler_params=_params(("parallel",)),
        name="ssm_pack",
    )(u_lb)


def _ssm_unpack_kernel(y_ref, u_ref, d_ref, o_ref):
    blk = _lane_block_ids()
    d = jnp.broadcast_to(d_ref[...], (BATCH, LANES))

    def body(it, carry):
        for pair in range(PACK_CHUNKS // 2):
            first = (it * (PACK_CHUNKS // 2) + pair) * 2 * BATCH
            rows = pl.ds(pl.multiple_of(first, 2 * BATCH), 2 * BATCH)
            y = [y_ref[gl, rows, :].astype(F32) for gl in range(GROUPS_PER_VREG)]
            for k in range(2):
                c = it * PACK_CHUNKS + 2 * pair + k
                for t_hi in range(SSM_T // BLOCKS_PER_VREG):
                    src = [y[gl][k * BATCH:(k + 1) * BATCH, t_hi * LANES:(t_hi + 1) * LANES]
                           for gl in range(GROUPS_PER_VREG)]
                    out = _block_transpose(src, blk)
                    for t_lo in range(BLOCKS_PER_VREG):
                        tok = _token_rows(c, t_hi * BLOCKS_PER_VREG + t_lo)
                        o_ref[tok, :] = _gelu_tanh(out[t_lo] + d * u_ref[tok, :])
        return carry

    lax.fori_loop(0, SSM_CHUNKS // PACK_CHUNKS, body, 0)


def _ssm_unpack(y_t, u_lb, d_skip, layer):
    slab = pl.BlockSpec((None, TOKENS, LANES), lambda i: (i, 0, 0))
    return pl.pallas_call(
        _ssm_unpack_kernel,
        grid=(D_MODEL // LANES,),
        in_specs=[pl.BlockSpec((GROUPS_PER_VREG, SSM_ROWS, SSM_LANES), lambda i: (i, 0, 0)),
                  slab,
                  pl.BlockSpec((None, 1, LANES), lambda i: (layer, 0, i))],
        out_specs=slab,
        out_shape=jax.ShapeDtypeStruct((D_MODEL // LANES, TOKENS, LANES), F32),
        compiler_params=_params(("parallel",)),
        name="ssm_unpack_gelu",
    )(y_t, u_lb, d_skip)


def _ssm_kernel(u_ref, wi_ref, ws_ref, wx_ref, a_ref, o_ref, s_ref, x_ref):
    p2 = 2 * SSM_STATE
    nt = (((1,), (1,)), ((), ()))
    for g in range(SSM_GB):
        s_ref[g] = jnp.dot(u_ref[g], ws_ref[g], preferred_element_type=F32)

    def step(c, carry):
        rf = pl.ds(pl.multiple_of(c * BATCH, BATCH), BATCH)
        rb = pl.ds(pl.multiple_of((SSM_CHUNKS - 1 - c) * BATCH, BATCH), BATCH)
        new = []
        for g in range(SSM_GB):
            xf, xfs, xb, xbs = carry[4 * g:4 * g + 4]
            x_ref[g, rf, 0:p2] = xf
            x_ref[g, rb, p2:2 * p2] = xb
            nxf = a_ref[g, 0] * xf + a_ref[g, 1] * xfs + s_ref[g, rf, 0:p2]
            nxfs = a_ref[g, 0] * xfs + a_ref[g, 2] * xf + s_ref[g, rf, p2:2 * p2]
            nxb = a_ref[g, 3] * xb + a_ref[g, 4] * xbs + s_ref[g, rb, 2 * p2:3 * p2]
            nxbs = a_ref[g, 3] * xbs + a_ref[g, 5] * xb + s_ref[g, rb, 3 * p2:4 * p2]
            new += [nxf, nxfs, nxb, nxbs]
        return tuple(new)

    zero = jnp.zeros((BATCH, p2), F32)
    lax.fori_loop(0, SSM_CHUNKS, step, (zero,) * (4 * SSM_GB))

    for g in range(SSM_GB):
        y = jnp.dot(u_ref[g], wi_ref[g], preferred_element_type=F32)
        y = y + lax.dot_general(x_ref[g].astype(BF16), wx_ref[g], nt, preferred_element_type=F32)
        o_ref[g] = y.astype(o_ref.dtype)


def _ssm(u_t, w_intra, w_state, w_inter_t, coeffs, layer):
    gb = SSM_GB

    def wspec(n):
        return pl.BlockSpec((None, gb, SSM_LANES, n), lambda i: (layer, i, 0, 0))

    return pl.pallas_call(
        _ssm_kernel,
        grid=(SSM_GROUPS // gb,),
        in_specs=[pl.BlockSpec((gb, SSM_ROWS, SSM_LANES), lambda i: (i, 0, 0)),
                  wspec(SSM_LANES), wspec(8 * SSM_STATE), wspec(4 * SSM_STATE),
                  pl.BlockSpec((None, gb, 6, BATCH, 2 * SSM_STATE), lambda i: (layer, i, 0, 0, 0))],
        out_specs=pl.BlockSpec((gb, SSM_ROWS, SSM_LANES), lambda i: (i, 0, 0)),
        out_shape=jax.ShapeDtypeStruct((SSM_GROUPS, SSM_ROWS, SSM_LANES), BF16),
        scratch_shapes=[pltpu.VMEM((gb, SSM_ROWS, 8 * SSM_STATE), F32),
                        pltpu.VMEM((gb, SSM_ROWS, 4 * SSM_STATE), F32)],
        compiler_params=_params(("parallel",)),
        name="ssm_core",
    )(u_t, w_intra, w_state, w_inter_t, coeffs)


def _glu_merge_kernel(a_ref, w_ref, b_ref, gs_ref, ret_ref, o_ref, af_ref, ab_ref):
    rows = BATCH * GLU_TL
    for k in range(D_MODEL // LANES):
        for b in range(BATCH):
            af_ref[b * GLU_TL:(b + 1) * GLU_TL, k * LANES:(k + 1) * LANES] = (
                a_ref[k, pl.ds(b, GLU_TL, stride=BATCH), :])
    ab_ref[...] = af_ref[...].astype(BF16)
    for n in range(D_MODEL // GLU_TN):
        cols = slice(n * GLU_TN, (n + 1) * GLU_TN)
        z = jnp.dot(ab_ref[...], w_ref[:, cols], preferred_element_type=F32) + b_ref[:, cols]
        ssm_out = af_ref[:, cols] * _sigmoid(z)
        gate = _sigmoid(gs_ref[:, :, cols].reshape(rows, GLU_TN).astype(F32))
        merged = ret_ref[:, :, cols].reshape(rows, GLU_TN).astype(F32) + gate * ssm_out
        o_ref[:, :, cols] = merged.astype(o_ref.dtype).reshape(BATCH, GLU_TL, GLU_TN)


def _glu_merge(a_lb, proj3, w_glu, b_glu, ret_g3, layer):
    rows = BATCH * GLU_TL
    tile = pl.BlockSpec((BATCH, GLU_TL, D_MODEL), lambda i: (0, i, 0))
    return pl.pallas_call(
        _glu_merge_kernel,
        grid=(SEQ // GLU_TL,),
        in_specs=[pl.BlockSpec((D_MODEL // LANES, rows, LANES), lambda i: (0, i, 0)),
                  pl.BlockSpec((None, D_MODEL, D_MODEL), lambda i: (layer, 0, 0),
                               pipeline_mode=pl.Buffered(1)),
                  pl.BlockSpec((None, 1, D_MODEL), lambda i: (layer, 0, 0)),
                  pl.BlockSpec((BATCH, GLU_TL, D_MODEL), lambda i: (0, i, COL_GATE_S // D_MODEL)),
                  tile],
        out_specs=tile,
        out_shape=jax.ShapeDtypeStruct((BATCH, SEQ, D_MODEL), BF16),
        scratch_shapes=[pltpu.VMEM((rows, D_MODEL), F32), pltpu.VMEM((rows, D_MODEL), BF16)],
        compiler_params=_params(("parallel",)),
        name="glu_merge",
    )(a_lb, w_glu, b_glu, proj3, ret_g3)


def _proj_residual_kernel(a_ref, w_ref, x_ref, o_ref):
    o_ref[...] = x_ref[...] + jnp.dot(a_ref[...], w_ref[...], preferred_element_type=F32)


def _proj_residual(a, w, x, layer):
    m, kdim = a.shape
    n = w.shape[-1]
    return pl.pallas_call(
        _proj_residual_kernel,
        grid=(m // TM, n // TN),
        in_specs=[pl.BlockSpec((TM, kdim), lambda i, j: (i, 0)),
                  pl.BlockSpec((None, kdim, TN), lambda i, j: (layer, 0, j)),
                  pl.BlockSpec((TM, TN), lambda i, j: (i, j))],
        out_specs=pl.BlockSpec((TM, TN), lambda i, j: (i, j)),
        out_shape=jax.ShapeDtypeStruct((m, n), F32),
        compiler_params=_params(("parallel", "parallel")),
        name="out_proj_residual",
    )(a, w, x)


def _ffn_kernel(x_ref, g_ref, wg_ref, wu_ref, wd_ref, gf_ref, o_ref, h_ref, acc_ref, *, final_norm):
    j = pl.program_id(1)

    @pl.when(j == 0)
    def _():
        h_ref[...] = _rms(x_ref[...], g_ref[...]).astype(BF16)
        acc_ref[...] = jnp.zeros_like(acc_ref)

    h = h_ref[...]
    gate = jnp.dot(h, wg_ref[...], preferred_element_type=F32)
    up = jnp.dot(h, wu_ref[...], preferred_element_type=F32)
    act = (_silu(gate) * up).astype(BF16)
    acc_ref[...] += jnp.dot(act, wd_ref[...], preferred_element_type=F32)

    @pl.when(j == pl.num_programs(1) - 1)
    def _():
        y = x_ref[...] + acc_ref[...]
        if final_norm:
            y = _rms(y, gf_ref[...])
        o_ref[...] = y


def _ffn(x, gain, w_gate, w_up, w_down, final_gain, layer, final_norm):
    m, d = x.shape
    f = w_gate.shape[-1]
    return pl.pallas_call(
        functools.partial(_ffn_kernel, final_norm=final_norm),
        grid=(m // FFN_TM, f // FFN_TF),
        in_specs=[pl.BlockSpec((FFN_TM, d), lambda i, j: (i, 0)),
                  pl.BlockSpec((None, 1, d), lambda i, j: (layer, 0, 0)),
                  pl.BlockSpec((None, d, FFN_TF), lambda i, j: (layer, 0, j)),
                  pl.BlockSpec((None, d, FFN_TF), lambda i, j: (layer, 0, j)),
                  pl.BlockSpec((None, FFN_TF, d), lambda i, j: (layer, j, 0)),
                  pl.BlockSpec((1, d), lambda i, j: (0, 0))],
        out_specs=pl.BlockSpec((FFN_TM, d), lambda i, j: (i, 0)),
        out_shape=jax.ShapeDtypeStruct((m, d), F32),
        scratch_shapes=[pltpu.VMEM((FFN_TM, d), BF16), pltpu.VMEM((FFN_TM, d), F32)],
        compiler_params=_params(("parallel", "arbitrary")),
        name="ffn",
    )(x, gain, w_gate, w_up, w_down, final_gain)


def _rotary_tables():
    half = RET_QK_DIM // 2
    inv = 1.0 / (ROPE_BASE ** (jnp.arange(half, dtype=F32) / half))
    ang = jnp.arange(SEQ, dtype=F32)[:, None] * inv[None, :]
    return jnp.cos(ang), jnp.sin(ang)


def kernel(x, ln_mix_g, w_in, ret_log_gamma, ssm_a_re, ssm_a_im, ssm_log_dt, ssm_b_re, ssm_b_im,
           ssm_c_re, ssm_c_im, ssm_d, w_glu, b_glu, w_out, ln_ffn_g, w_ffn_gate, w_ffn_up,
           w_ffn_down, ln_final_g):
    depth = w_in.shape[0]
    cos, sin = _rotary_tables()
    row = lambda p: p.astype(F32).reshape(depth, 1, D_MODEL)
    mix_gain, ffn_gain, d_skip, glu_bias = row(ln_mix_g), row(ln_ffn_g), row(ssm_d), row(b_glu)
    final_gain = ln_final_g.astype(F32).reshape(1, D_MODEL)
    log_gamma = ret_log_gamma.astype(F32)
    w_in_b, w_glu_b, w_out_b = w_in.astype(BF16), w_glu.astype(BF16), w_out.astype(BF16)
    w_gate_b, w_up_b, w_down_b = (w_ffn_gate.astype(BF16), w_ffn_up.astype(BF16),
                                  w_ffn_down.astype(BF16))
    s_cat, i_cat, c_tile, coeffs = _ssm_factors(ssm_a_re, ssm_a_im, ssm_log_dt, ssm_b_re, ssm_b_im,
                                                ssm_c_re, ssm_c_im)
    w_intra, w_state, w_inter_t = _ssm_weights(s_cat, i_cat, c_tile)

    xt = x.reshape(TOKENS, D_MODEL)
    for i in range(depth):
        proj3, u_lb = _norm_proj(xt.reshape(BATCH, SEQ, D_MODEL), mix_gain, w_in_b, i)
        ret_g3 = _retention(proj3, log_gamma, cos, sin, i)
        y_t = _ssm(_ssm_pack(u_lb), w_intra, w_state, w_inter_t, coeffs, i)
        a_lb = _ssm_unpack(y_t, u_lb, d_skip, i)
        merged3 = _glu_merge(a_lb, proj3, w_glu_b, glu_bias, ret_g3, i)
        xt = _proj_residual(merged3.reshape(TOKENS, D_MODEL), w_out_b, xt, i)
        xt = _ffn(xt, ffn_gain, w_gate_b, w_up_b, w_down_b, final_gain, i,
                  final_norm=(i == depth - 1))
    return xt.reshape(BATCH, SEQ, D_MODEL)
```

```python
import functools
import math

import numpy as np
import jax
import jax.numpy as jnp
from jax import lax
from jax.experimental import pallas as pl
from jax.experimental.pallas import tpu as pltpu

F32 = jnp.float32
BF16 = jnp.bfloat16

D_MODEL = 2048
BATCH = 8
SEQ = 2048
TOKENS = BATCH * SEQ
RET_HEADS = 4
RET_QK_DIM = 256
RET_V_DIM = D_MODEL // RET_HEADS
RET_QK_WIDTH = RET_HEADS * RET_QK_DIM
ROPE_BASE = 10000.0
SSM_GROUP = 16
SSM_GROUPS = D_MODEL // SSM_GROUP
SSM_STATE = 64
D_FF = ((8 * D_MODEL // 3 + 255) // 256) * 256
IN_WIDTH = 2 * RET_QK_WIDTH + 5 * D_MODEL
EPS = 1e-6

COL_Q = 0
COL_K = RET_QK_WIDTH
COL_V = 2 * RET_QK_WIDTH
COL_G = COL_V + D_MODEL
COL_U = COL_G + D_MODEL
COL_GATE_R = COL_U + D_MODEL
COL_GATE_S = COL_GATE_R + D_MODEL

LANES = 128
RET_CHUNK = 256
SSM_T = 16
SSM_CHUNKS = SEQ // SSM_T
SSM_ROWS = SSM_CHUNKS * BATCH
SSM_LANES = SSM_T * SSM_GROUP
SSM_GB = 4
SSM_WGB = 8
GROUPS_PER_VREG = LANES // SSM_GROUP
BLOCKS_PER_VREG = LANES // SSM_GROUP

VMEM_LIMIT = 56 * 1024 * 1024

TM = 1024
PROJ_TL = 128
TN = 1024
GLU_TL = 64
GLU_TN = 512
PACK_CHUNKS = 4
FFN_TM = 512
FFN_TF = 512


def _params(sem):
    return pltpu.CompilerParams(dimension_semantics=sem, vmem_limit_bytes=VMEM_LIMIT)


def _rms(x, g):
    return x * lax.rsqrt(jnp.mean(x * x, axis=-1, keepdims=True) + EPS) * g


def _sigmoid(x):
    return 0.5 * jnp.tanh(0.5 * x) + 0.5


def _silu(x):
    h = 0.5 * x
    return h + h * jnp.tanh(h)


def _gelu_tanh(x):
    return 0.5 * x * (1.0 + jnp.tanh(math.sqrt(2.0 / math.pi) * (x + 0.044715 * (x * x * x))))


def _norm_proj_kernel(x_ref, g_ref, w_ref, o_ref, ut_ref, h_ref, p_ref):
    j = pl.program_id(1)
    rows = BATCH * PROJ_TL

    @pl.when(j == 0)
    def _():
        h_ref[...] = _rms(x_ref[...].reshape(rows, D_MODEL), g_ref[...]).astype(BF16)

    u_lo, u_hi = COL_U // TN, COL_GATE_R // TN - 1

    def project():
        acc = jnp.dot(h_ref[...], w_ref[...], preferred_element_type=F32)
        o_ref[...] = acc.astype(o_ref.dtype).reshape(BATCH, PROJ_TL, TN)
        return acc

    def stash(acc, slot):
        for k in range(TN // LANES):
            for b in range(BATCH):
                p_ref[slot, k, pl.ds(b, PROJ_TL, stride=BATCH), :] = (
                    acc[b * PROJ_TL:(b + 1) * PROJ_TL, k * LANES:(k + 1) * LANES])

    def relayout(slot):
        blk = _lane_block_ids()
        for k in range(TN // LANES):
            for pair in range(PROJ_TL // SSM_T // 2):
                halves = []
                for c in (2 * pair, 2 * pair + 1):
                    parts = []
                    for s_hi in range(SSM_T // BLOCKS_PER_VREG):
                        v = [p_ref[slot, k, _token_rows(c, s_hi * BLOCKS_PER_VREG + s_lo), :]
                             for s_lo in range(BLOCKS_PER_VREG)]
                        parts.append(_block_transpose(v, blk))
                    halves.append([jnp.concatenate([p[gl] for p in parts], axis=1)
                                   for gl in range(GROUPS_PER_VREG)])
                for gl in range(GROUPS_PER_VREG):
                    ut_ref[k * GROUPS_PER_VREG + gl, pair * 2 * BATCH:(pair + 1) * 2 * BATCH, :] = (
                        jnp.concatenate([halves[0][gl], halves[1][gl]], axis=0).astype(BF16))

    relayout_step = (j > u_lo) & (j <= u_hi + 1)

    @pl.when(jnp.logical_not(relayout_step))
    def _():
        acc = project()

        @pl.when(j == u_lo)
        def _():
            stash(acc, 0)

    @pl.when(relayout_step)
    def _():
        acc = project()
        relayout((j - u_lo - 1) % 2)

        @pl.when(j <= u_hi)
        def _():
            stash(acc, (j - u_lo) % 2)


def _norm_proj(x3, gain, w, layer):
    n = w.shape[-1]
    u_lo, u_hi = COL_U // TN + 1, COL_GATE_R // TN
    rows = BATCH * PROJ_TL
    return pl.pallas_call(
        _norm_proj_kernel,
        grid=(SEQ // PROJ_TL, n // TN),
        in_specs=[pl.BlockSpec((BATCH, PROJ_TL, D_MODEL), lambda i, j: (0, i, 0)),
                  pl.BlockSpec((None, 1, D_MODEL), lambda i, j: (layer, 0, 0)),
                  pl.BlockSpec((None, D_MODEL, TN), lambda i, j: (layer, 0, j))],
        out_specs=[pl.BlockSpec((BATCH, PROJ_TL, TN), lambda i, j: (0, i, j)),
                   pl.BlockSpec((TN // SSM_GROUP, rows // SSM_T, SSM_LANES),
                                lambda i, j: (jnp.clip(j, u_lo, u_hi) - u_lo, i, 0))],
        out_shape=[jax.ShapeDtypeStruct((BATCH, SEQ, n), BF16),
                   jax.ShapeDtypeStruct((SSM_GROUPS, SSM_ROWS, SSM_LANES), BF16)],
        scratch_shapes=[pltpu.VMEM((rows, D_MODEL), BF16),
                        pltpu.VMEM((2, TN // LANES, rows, LANES), F32)],
        compiler_params=_params(("parallel", "arbitrary")),
        name="norm_in_proj",
    )(x3, gain, w)


def _retention_kernel(lg_ref, q_ref, k_ref, v_ref, g_ref, gr_ref, cos_ref, sin_ref, o_ref,
                      qs_ref, ks_ref, ys_ref, sf_ref, sb_ref, *, layer):
    c_len = RET_CHUNK
    n_chunks = SEQ // c_len
    half = RET_QK_DIM // 2
    head = pl.program_id(1)
    lg_f = lg_ref[layer, 0, head]
    lg_b = lg_ref[layer, 1, head]

    cos = cos_ref[...]
    sin = sin_ref[...]
    q = q_ref[...].astype(F32)
    k = k_ref[...].astype(F32) * (RET_QK_DIM ** -0.5)
    qs_ref[:, :half] = q[:, :half] * cos - q[:, half:] * sin
    qs_ref[:, half:] = q[:, :half] * sin + q[:, half:] * cos
    ks_ref[:, :half] = k[:, :half] * cos - k[:, half:] * sin
    ks_ref[:, half:] = k[:, :half] * sin + k[:, half:] * cos

    t_row = lax.broadcasted_iota(jnp.int32, (c_len, RET_QK_DIM), 0).astype(F32)
    q_scale_f = jnp.exp(lg_f * (t_row + 1.0))
    q_scale_b = jnp.exp(lg_b * (c_len - t_row))
    k_scale_f = jnp.exp(lg_f * (c_len - 1.0 - t_row))
    k_scale_b = jnp.exp(lg_b * t_row)
    diff = (lax.broadcasted_iota(jnp.int32, (c_len, c_len), 0)
            - lax.broadcasted_iota(jnp.int32, (c_len, c_len), 1)).astype(F32)
    dmat = jnp.exp(jnp.where(diff >= 0, lg_f * diff, -lg_b * diff))
    decay_f = jnp.exp(jnp.full((1, RET_V_DIM), lg_f * c_len, F32))
    decay_b = jnp.exp(jnp.full((1, RET_V_DIM), lg_b * c_len, F32))

    sf_ref[...] = jnp.zeros_like(sf_ref)
    sb_ref[...] = jnp.zeros_like(sb_ref)

    def rows(c):
        return pl.ds(pl.multiple_of(c * c_len, c_len), c_len)

    def forward(c, carry):
        r = rows(c)
        qc = qs_ref[r, :]
        kc = ks_ref[r, :]
        vc = v_ref[r, :]
        scores = lax.dot_general(qc.astype(BF16), kc.astype(BF16), (((1,), (1,)), ((), ())),
                                 preferred_element_type=F32) * dmat
        y = jnp.dot(scores.astype(BF16), vc, preferred_element_type=F32)
        y = y + jnp.dot((qc * q_scale_f).astype(BF16), sf_ref[...].astype(BF16),
                        preferred_element_type=F32)
        ys_ref[r, :] = y
        kv = lax.dot_general((kc * k_scale_f).astype(BF16), vc, (((0,), (0,)), ((), ())),
                             preferred_element_type=F32)
        sf_ref[...] = decay_f * sf_ref[...] + kv
        return carry

    lax.fori_loop(0, n_chunks, forward, 0)

    def backward(i, carry):
        c = n_chunks - 1 - i
        r = rows(c)
        qc = qs_ref[r, :]
        kc = ks_ref[r, :]
        vc = v_ref[r, :]
        y = ys_ref[r, :] + jnp.dot((qc * q_scale_b).astype(BF16), sb_ref[...].astype(BF16),
                                   preferred_element_type=F32)
        kv = lax.dot_general((kc * k_scale_b).astype(BF16), vc, (((0,), (0,)), ((), ())),
                             preferred_element_type=F32)
        sb_ref[...] = decay_b * sb_ref[...] + kv
        yn = y * lax.rsqrt(jnp.mean(y * y, axis=-1, keepdims=True) + EPS)
        g = g_ref[r, :].astype(F32)
        gr = gr_ref[r, :].astype(F32)
        o_ref[r, :] = (_sigmoid(gr) * (_silu(g) * yn)).astype(o_ref.dtype)
        return carry

    lax.fori_loop(0, n_chunks, backward, 0)


def _retention(proj3, log_gamma, cos, sin, layer):
    qk_blk = RET_QK_DIM
    v_blk = RET_V_DIM
    return pl.pallas_call(
        functools.partial(_retention_kernel, layer=layer),
        grid=(BATCH, RET_HEADS),
        in_specs=[
            pl.BlockSpec(memory_space=pltpu.SMEM),
            pl.BlockSpec((None, SEQ, qk_blk), lambda b, h: (b, 0, COL_Q // qk_blk + h)),
            pl.BlockSpec((None, SEQ, qk_blk), lambda b, h: (b, 0, COL_K // qk_blk + h)),
            pl.BlockSpec((None, SEQ, v_blk), lambda b, h: (b, 0, COL_V // v_blk + h)),
            pl.BlockSpec((None, SEQ, v_blk), lambda b, h: (b, 0, COL_G // v_blk + h)),
            pl.BlockSpec((None, SEQ, v_blk), lambda b, h: (b, 0, COL_GATE_R // v_blk + h)),
            pl.BlockSpec((SEQ, RET_QK_DIM // 2), lambda b, h: (0, 0)),
            pl.BlockSpec((SEQ, RET_QK_DIM // 2), lambda b, h: (0, 0)),
        ],
        out_specs=pl.BlockSpec((None, SEQ, v_blk), lambda b, h: (b, 0, h)),
        out_shape=jax.ShapeDtypeStruct((BATCH, SEQ, D_MODEL), BF16),
        scratch_shapes=[pltpu.VMEM((SEQ, RET_QK_DIM), F32),
                        pltpu.VMEM((SEQ, RET_QK_DIM), F32),
                        pltpu.VMEM((SEQ, RET_V_DIM), F32),
                        pltpu.VMEM((RET_QK_DIM, RET_V_DIM), F32),
                        pltpu.VMEM((RET_QK_DIM, RET_V_DIM), F32)],
        compiler_params=_params(("parallel", "arbitrary")),
        name="retention",
    )(log_gamma, proj3, proj3, proj3, proj3, proj3, cos, sin)


def _ssm_factors(a_re, a_im, log_dt, b_re, b_im, c_re, c_im):
    t_len, p = SSM_T, SSM_STATE
    depth = a_re.shape[0]
    ar_t = np.arange(t_len, dtype=np.float32)
    n_state = np.stack([t_len - 1.0 - ar_t, ar_t])
    n_out = np.stack([ar_t + 1.0, t_len - ar_t])
    dt = jnp.exp(log_dt)[..., None]
    lr, li = a_re, a_im
    lrdt, lidt = lr * dt, li * dt

    def powers(n):
        e = n[None, :, None, :, None]
        mag = jnp.exp(e * lrdt[:, :, :, None, :])
        ang = e * lidt[:, :, :, None, :]
        return mag * jnp.cos(ang), mag * jnp.sin(ang)

    mag1 = jnp.exp(lrdt)
    lbr, lbi = mag1 * jnp.cos(lidt), mag1 * jnp.sin(lidt)
    den = lr * lr + li * li
    nr, ni = lbr - 1.0, lbi
    fr, fi = (nr * lr + ni * li) / den, (ni * lr - nr * li) / den
    bt_r = jnp.swapaxes(b_re, -1, -2)
    bt_i = jnp.swapaxes(b_im, -1, -2)
    bbr = fr[..., None, :] * bt_r - fi[..., None, :] * bt_i
    bbi = fr[..., None, :] * bt_i + fi[..., None, :] * bt_r

    pr, pi = powers(n_state)
    qr, qi = powers(n_out)
    cat = lambda a, b: jnp.concatenate([a, b], axis=-1)
    fac = jnp.stack([cat(pr, pr), cat(pi, pi), cat(bbr, bbi), cat(-bbi, bbr),
                     cat(qr, qr), cat(qi, qi), cat(c_re, -c_im), cat(-c_im, -c_re)],
                    axis=3).astype(F32)

    mag_t = jnp.exp(t_len * lrdt)
    ar, ai = mag_t * jnp.cos(t_len * lidt), mag_t * jnp.sin(t_len * lidt)
    a1 = jnp.concatenate([ar, ar], axis=-1)
    a2 = jnp.concatenate([-ai, ai], axis=-1)
    coeffs = jnp.stack([a1, a2, -a2], axis=3)
    coeffs = coeffs.transpose(0, 2, 1, 3, 4).reshape(depth, SSM_GROUPS, 6, 2 * p)
    coeffs = jnp.broadcast_to(coeffs[:, :, :, None, :], (depth, SSM_GROUPS, 6, BATCH, 2 * p))
    return fac, coeffs.astype(F32)


def _expand_factors(f_ref, g):
    p1, p2, b1, b2, q1, q2, c1, c2 = [f_ref[g, n] for n in range(8)]
    s_cat = jnp.concatenate([p1[s:s + 1, :] * b1 + p2[s:s + 1, :] * b2 for s in range(SSM_T)], axis=0)
    i_cat = jnp.concatenate([q1[t:t + 1, :] * c1 + q2[t:t + 1, :] * c2 for t in range(SSM_T)], axis=0)
    c_tile = jnp.concatenate([c1] * SSM_T, axis=0)
    return s_cat, i_cat, c_tile


def _ssm_weights_kernel(ff_ref, fb_ref, wi_ref, ws_ref, wx_ref):
    nt = (((1,), (1,)), ((), ()))
    blk = lax.broadcasted_iota(jnp.int32, (SSM_LANES, SSM_LANES), 1) // SSM_GROUP
    for g in range(SSM_WGB):
        sf, i_f, cf = _expand_factors(ff_ref, g)
        sb, i_b, cb = _expand_factors(fb_ref, g)
        kf = lax.dot_general(sf, cf, nt, precision=lax.Precision.HIGHEST,
                             preferred_element_type=F32)
        kb = lax.dot_general(sb, cb, nt, precision=lax.Precision.HIGHEST,
                             preferred_element_type=F32)
        acc = jnp.zeros((SSM_LANES, SSM_LANES), F32)
        for t in range(SSM_T):
            up = (SSM_T - 1 - t) * SSM_GROUP
            down = t * SSM_GROUP
            parts_f = [kf[up:, :]] + ([jnp.zeros((up, SSM_LANES), F32)] if up else [])
            parts_b = ([jnp.zeros((down, SSM_LANES), F32)] if down else []) + [kb[:SSM_LANES - down, :]]
            shifted = jnp.concatenate(parts_f, axis=0) + jnp.concatenate(parts_b, axis=0)
            acc = jnp.where(blk == t, shifted, acc)
        wi_ref[g] = acc.astype(wi_ref.dtype)
        ws_ref[g] = jnp.concatenate([sf, pltpu.roll(sf, SSM_STATE, axis=1),
                                     sb, pltpu.roll(sb, SSM_STATE, axis=1)], axis=1).astype(ws_ref.dtype)
        wx_ref[g] = jnp.concatenate([i_f, i_b], axis=1).astype(wx_ref.dtype)


def _ssm_weights(fac):
    depth = fac.shape[0]
    gb = SSM_WGB
    blk = (None, None, gb) + fac.shape[3:]
    fwd = pl.BlockSpec(blk, lambda l, i: (l, 0, i, 0, 0, 0))
    bwd = pl.BlockSpec(blk, lambda l, i: (l, 1, i, 0, 0, 0))

    def out(n):
        return (pl.BlockSpec((None, gb, SSM_LANES, n), lambda l, i: (l, i, 0, 0)),
                jax.ShapeDtypeStruct((depth, SSM_GROUPS, SSM_LANES, n), BF16))

    specs, shapes = zip(out(SSM_LANES), out(8 * SSM_STATE), out(4 * SSM_STATE))
    return pl.pallas_call(
        _ssm_weights_kernel,
        grid=(depth, SSM_GROUPS // gb),
        in_specs=[fwd, bwd],
        out_specs=list(specs),
        out_shape=list(shapes),
        compiler_params=_params(("parallel", "parallel")),
        name="ssm_weights",
    )(fac, fac)


def _lane_block_ids():
    return lax.broadcasted_iota(jnp.int32, (BATCH, LANES), 1) // SSM_GROUP


def _token_rows(c, t):
    first = (c * SSM_T + t) * BATCH
    return pl.ds(first if isinstance(first, int) else pl.multiple_of(first, BATCH), BATCH)


def _block_transpose(src, blk):
    n = BLOCKS_PER_VREG
    rolled = []
    for d in range(n):
        m = src[d % n]
        for a in range(1, n):
            m = jnp.where(blk == a, src[(a + d) % n], m)
        rolled.append(pltpu.roll(m, d * SSM_GROUP, axis=1) if d else m)
    out = []
    for a in range(n):
        t = rolled[(0 - a) % n]
        for b in range(1, n):
            t = jnp.where(blk == b, rolled[(b - a) % n], t)
        out.append(t)
    return out


def _ssm_unpack_kernel(y_ref, o_ref):
    blk = _lane_block_ids()

    def body(it, carry):
        for pair in range(PACK_CHUNKS // 2):
            first = (it * (PACK_CHUNKS // 2) + pair) * 2 * BATCH
            rows = pl.ds(pl.multiple_of(first, 2 * BATCH), 2 * BATCH)
            y = [y_ref[gl, rows, :].astype(F32) for gl in range(GROUPS_PER_VREG)]
            for k in range(2):
                c = it * PACK_CHUNKS + 2 * pair + k
                for t_hi in range(SSM_T // BLOCKS_PER_VREG):
                    src = [y[gl][k * BATCH:(k + 1) * BATCH, t_hi * LANES:(t_hi + 1) * LANES]
                           for gl in range(GROUPS_PER_VREG)]
                    out = _block_transpose(src, blk)
                    for t_lo in range(BLOCKS_PER_VREG):
                        o_ref[_token_rows(c, t_hi * BLOCKS_PER_VREG + t_lo), :] = out[t_lo]
        return carry

    lax.fori_loop(0, SSM_CHUNKS // PACK_CHUNKS, body, 0)


def _ssm_unpack(a_t):
    return pl.pallas_call(
        _ssm_unpack_kernel,
        grid=(D_MODEL // LANES,),
        in_specs=[pl.BlockSpec((GROUPS_PER_VREG, SSM_ROWS, SSM_LANES), lambda i: (i, 0, 0))],
        out_specs=pl.BlockSpec((None, TOKENS, LANES), lambda i: (i, 0, 0)),
        out_shape=jax.ShapeDtypeStruct((D_MODEL // LANES, TOKENS, LANES), F32),
        compiler_params=_params(("parallel",)),
        name="ssm_unpack",
    )(a_t)


def _ssm_kernel(u_ref, wi_ref, ws_ref, wx_ref, a_ref, d_ref, o_ref, s_ref, x_ref):
    p2 = 2 * SSM_STATE
    nt = (((1,), (1,)), ((), ()))
    for g in range(SSM_GB):
        s_ref[g] = jnp.dot(u_ref[g], ws_ref[g], preferred_element_type=F32)

    def step(c, carry):
        rf = pl.ds(pl.multiple_of(c * BATCH, BATCH), BATCH)
        rb = pl.ds(pl.multiple_of((SSM_CHUNKS - 1 - c) * BATCH, BATCH), BATCH)
        new = []
        for g in range(SSM_GB):
            xf, xfs, xb, xbs = carry[4 * g:4 * g + 4]
            x_ref[g, rf, 0:p2] = xf
            x_ref[g, rb, p2:2 * p2] = xb
            nxf = a_ref[g, 0] * xf + a_ref[g, 1] * xfs + s_ref[g, rf, 0:p2]
            nxfs = a_ref[g, 0] * xfs + a_ref[g, 2] * xf + s_ref[g, rf, p2:2 * p2]
            nxb = a_ref[g, 3] * xb + a_ref[g, 4] * xbs + s_ref[g, rb, 2 * p2:3 * p2]
            nxbs = a_ref[g, 3] * xbs + a_ref[g, 5] * xb + s_ref[g, rb, 3 * p2:4 * p2]
            new += [nxf, nxfs, nxb, nxbs]
        return tuple(new)

    zero = jnp.zeros((BATCH, p2), F32)
    lax.fori_loop(0, SSM_CHUNKS, step, (zero,) * (4 * SSM_GB))

    for g in range(SSM_GB):
        y = jnp.dot(u_ref[g], wi_ref[g], preferred_element_type=F32)
        y = y + lax.dot_general(x_ref[g].astype(BF16), wx_ref[g], nt, preferred_element_type=F32)
        o_ref[g] = _gelu_tanh(y + d_ref[g] * u_ref[g].astype(F32)).astype(o_ref.dtype)


def _ssm(u_t, w_intra, w_state, w_inter_t, coeffs, d_t, layer):
    gb = SSM_GB

    def wspec(n):
        return pl.BlockSpec((None, gb, SSM_LANES, n), lambda i: (layer, i, 0, 0))

    return pl.pallas_call(
        _ssm_kernel,
        grid=(SSM_GROUPS // gb,),
        in_specs=[pl.BlockSpec((gb, SSM_ROWS, SSM_LANES), lambda i: (i, 0, 0)),
                  wspec(SSM_LANES), wspec(8 * SSM_STATE), wspec(4 * SSM_STATE),
                  pl.BlockSpec((None, gb, 6, BATCH, 2 * SSM_STATE), lambda i: (layer, i, 0, 0, 0)),
                  pl.BlockSpec((None, gb, 1, SSM_LANES), lambda i: (layer, i, 0, 0))],
        out_specs=pl.BlockSpec((gb, SSM_ROWS, SSM_LANES), lambda i: (i, 0, 0)),
        out_shape=jax.ShapeDtypeStruct((SSM_GROUPS, SSM_ROWS, SSM_LANES), BF16),
        scratch_shapes=[pltpu.VMEM((gb, SSM_ROWS, 8 * SSM_STATE), F32),
                        pltpu.VMEM((gb, SSM_ROWS, 4 * SSM_STATE), F32)],
        compiler_params=_params(("parallel",)),
        name="ssm_core",
    )(u_t, w_intra, w_state, w_inter_t, coeffs, d_t)


def _glu_merge_kernel(a_ref, w_ref, b_ref, gs_ref, ret_ref, o_ref, af_ref, ab_ref):
    rows = BATCH * GLU_TL
    for k in range(D_MODEL // LANES):
        for b in range(BATCH):
            af_ref[b * GLU_TL:(b + 1) * GLU_TL, k * LANES:(k + 1) * LANES] = (
                a_ref[k, pl.ds(b, GLU_TL, stride=BATCH), :])
    ab_ref[...] = af_ref[...].astype(BF16)
    for n in range(D_MODEL // GLU_TN):
        cols = slice(n * GLU_TN, (n + 1) * GLU_TN)
        z = jnp.dot(ab_ref[...], w_ref[:, cols], preferred_element_type=F32) + b_ref[:, cols]
        ssm_out = af_ref[:, cols] * _sigmoid(z)
        gate = _sigmoid(gs_ref[:, :, cols].reshape(rows, GLU_TN).astype(F32))
        merged = ret_ref[:, :, cols].reshape(rows, GLU_TN).astype(F32) + gate * ssm_out
        o_ref[:, :, cols] = merged.astype(o_ref.dtype).reshape(BATCH, GLU_TL, GLU_TN)


def _glu_merge(a_lb, proj3, w_glu, b_glu, ret_g3, layer):
    rows = BATCH * GLU_TL
    tile = pl.BlockSpec((BATCH, GLU_TL, D_MODEL), lambda i: (0, i, 0))
    return pl.pallas_call(
        _glu_merge_kernel,
        grid=(SEQ // GLU_TL,),
        in_specs=[pl.BlockSpec((D_MODEL // LANES, rows, LANES), lambda i: (0, i, 0)),
                  pl.BlockSpec((None, D_MODEL, D_MODEL), lambda i: (layer, 0, 0),
                               pipeline_mode=pl.Buffered(1)),
                  pl.BlockSpec((None, 1, D_MODEL), lambda i: (layer, 0, 0)),
                  pl.BlockSpec((BATCH, GLU_TL, D_MODEL), lambda i: (0, i, COL_GATE_S // D_MODEL)),
                  tile],
        out_specs=tile,
        out_shape=jax.ShapeDtypeStruct((BATCH, SEQ, D_MODEL), BF16),
        scratch_shapes=[pltpu.VMEM((rows, D_MODEL), F32), pltpu.VMEM((rows, D_MODEL), BF16)],
        compiler_params=_params(("parallel",)),
        name="glu_merge",
    )(a_lb, w_glu, b_glu, proj3, ret_g3)


def _proj_residual_kernel(a_ref, w_ref, x_ref, o_ref):
    o_ref[...] = x_ref[...] + jnp.dot(a_ref[...], w_ref[...], preferred_element_type=F32)


def _proj_residual(a, w, x, layer):
    m, kdim = a.shape
    n = w.shape[-1]
    return pl.pallas_call(
        _proj_residual_kernel,
        grid=(m // TM, n // TN),
        in_specs=[pl.BlockSpec((TM, kdim), lambda i, j: (i, 0)),
                  pl.BlockSpec((None, kdim, TN), lambda i, j: (layer, 0, j)),
                  pl.BlockSpec((TM, TN), lambda i, j: (i, j))],
        out_specs=pl.BlockSpec((TM, TN), lambda i, j: (i, j)),
        out_shape=jax.ShapeDtypeStruct((m, n), F32),
        compiler_params=_params(("parallel", "parallel")),
        name="out_proj_residual",
    )(a, w, x)


def _ffn_kernel(x_ref, g_ref, wg_ref, wu_ref, wd_ref, gf_ref, o_ref, h_ref, acc_ref, *, final_norm):
    j = pl.program_id(1)

    @pl.when(j == 0)
    def _():
        h_ref[...] = _rms(x_ref[...], g_ref[...]).astype(BF16)
        acc_ref[...] = jnp.zeros_like(acc_ref)

    h = h_ref[...]
    gate = jnp.dot(h, wg_ref[...], preferred_element_type=F32)
    up = jnp.dot(h, wu_ref[...], preferred_element_type=F32)
    act = (_silu(gate) * up).astype(BF16)
    acc_ref[...] += jnp.dot(act, wd_ref[...], preferred_element_type=F32)

    @pl.when(j == pl.num_programs(1) - 1)
    def _():
        y = x_ref[...] + acc_ref[...]
        if final_norm:
            y = _rms(y, gf_ref[...])
        o_ref[...] = y


def _ffn(x, gain, w_gate, w_up, w_down, final_gain, layer, final_norm):
    m, d = x.shape
    f = w_gate.shape[-1]
    return pl.pallas_call(
        functools.partial(_ffn_kernel, final_norm=final_norm),
        grid=(m // FFN_TM, f // FFN_TF),
        in_specs=[pl.BlockSpec((FFN_TM, d), lambda i, j: (i, 0)),
                  pl.BlockSpec((None, 1, d), lambda i, j: (layer, 0, 0)),
                  pl.BlockSpec((None, d, FFN_TF), lambda i, j: (layer, 0, j)),
                  pl.BlockSpec((None, d, FFN_TF), lambda i, j: (layer, 0, j)),
                  pl.BlockSpec((None, FFN_TF, d), lambda i, j: (layer, j, 0)),
                  pl.BlockSpec((1, d), lambda i, j: (0, 0))],
        out_specs=pl.BlockSpec((FFN_TM, d), lambda i, j: (i, 0)),
        out_shape=jax.ShapeDtypeStruct((m, d), F32),
        scratch_shapes=[pltpu.VMEM((FFN_TM, d), BF16), pltpu.VMEM((FFN_TM, d), F32)],
        compiler_params=_params(("parallel", "arbitrary")),
        name="ffn",
    )(x, gain, w_gate, w_up, w_down, final_gain)


def _rotary_tables():
    half = RET_QK_DIM // 2
    inv = 1.0 / (ROPE_BASE ** (jnp.arange(half, dtype=F32) / half))
    ang = jnp.arange(SEQ, dtype=F32)[:, None] * inv[None, :]
    return jnp.cos(ang), jnp.sin(ang)


def kernel(x, ln_mix_g, w_in, ret_log_gamma, ssm_a_re, ssm_a_im, ssm_log_dt, ssm_b_re, ssm_b_im,
           ssm_c_re, ssm_c_im, ssm_d, w_glu, b_glu, w_out, ln_ffn_g, w_ffn_gate, w_ffn_up,
           w_ffn_down, ln_final_g):
    depth = w_in.shape[0]
    cos, sin = _rotary_tables()
    row = lambda p: p.astype(F32).reshape(depth, 1, D_MODEL)
    mix_gain, ffn_gain, glu_bias = row(ln_mix_g), row(ln_ffn_g), row(b_glu)
    d_t = jnp.tile(ssm_d.astype(F32).reshape(depth, SSM_GROUPS, 1, SSM_GROUP), (1, 1, 1, SSM_T))
    final_gain = ln_final_g.astype(F32).reshape(1, D_MODEL)
    log_gamma = ret_log_gamma.astype(F32)
    w_in_b, w_glu_b, w_out_b = w_in.astype(BF16), w_glu.astype(BF16), w_out.astype(BF16)
    w_gate_b, w_up_b, w_down_b = (w_ffn_gate.astype(BF16), w_ffn_up.astype(BF16),
                                  w_ffn_down.astype(BF16))
    fac, coeffs = _ssm_factors(ssm_a_re, ssm_a_im, ssm_log_dt, ssm_b_re, ssm_b_im,
                               ssm_c_re, ssm_c_im)
    w_intra, w_state, w_inter_t = _ssm_weights(fac)

    xt = x.reshape(TOKENS, D_MODEL)
    for i in range(depth):
        proj3, u_t = _norm_proj(xt.reshape(BATCH, SEQ, D_MODEL), mix_gain, w_in_b, i)
        ret_g3 = _retention(proj3, log_gamma, cos, sin, i)
        a_lb = _ssm_unpack(_ssm(u_t, w_intra, w_state, w_inter_t, coeffs, d_t, i))
        merged3 = _glu_merge(a_lb, proj3, w_glu_b, glu_bias, ret_g3, i)
        xt = _proj_residual(merged3.reshape(TOKENS, D_MODEL), w_out_b, xt, i)
        xt = _ffn(xt, ffn_gain, w_gate_b, w_up_b, w_down_b, final_gain, i,
                  final_norm=(i == depth - 1))
    return xt.reshape(BATCH, SEQ, D_MODEL)
```

```python
import functools
import math

import numpy as np
import jax
import jax.numpy as jnp
from jax import lax
from jax.experimental import pallas as pl
from jax.experimental.pallas import tpu as pltpu

F32 = jnp.float32
BF16 = jnp.bfloat16

D_MODEL = 2048
BATCH = 8
SEQ = 2048
TOKENS = BATCH * SEQ
RET_HEADS = 4
RET_QK_DIM = 256
RET_V_DIM = D_MODEL // RET_HEADS
RET_QK_WIDTH = RET_HEADS * RET_QK_DIM
ROPE_BASE = 10000.0
SSM_GROUP = 16
SSM_GROUPS = D_MODEL // SSM_GROUP
SSM_STATE = 64
D_FF = ((8 * D_MODEL // 3 + 255) // 256) * 256
IN_WIDTH = 2 * RET_QK_WIDTH + 5 * D_MODEL
EPS = 1e-6

COL_Q = 0
COL_K = RET_QK_WIDTH
COL_V = 2 * RET_QK_WIDTH
COL_G = COL_V + D_MODEL
COL_U = COL_G + D_MODEL
COL_GATE_R = COL_U + D_MODEL
COL_GATE_S = COL_GATE_R + D_MODEL

LANES = 128
RET_CHUNK = 256
SSM_T = 16
SSM_CHUNKS = SEQ // SSM_T
SSM_ROWS = SSM_CHUNKS * BATCH
SSM_LANES = SSM_T * SSM_GROUP
SSM_GB = 4
SSM_WGB = 8
GROUPS_PER_VREG = LANES // SSM_GROUP
BLOCKS_PER_VREG = LANES // SSM_GROUP

VMEM_LIMIT = 56 * 1024 * 1024

TM = 1024
PROJ_TL = 128
TN = 1024
GLU_TL = 64
GLU_TN = 512
PACK_CHUNKS = 4
FFN_TM = 1024
FFN_TF = 512


def _params(sem):
    return pltpu.CompilerParams(dimension_semantics=sem, vmem_limit_bytes=VMEM_LIMIT)


def _rms(x, g):
    return x * lax.rsqrt(jnp.mean(x * x, axis=-1, keepdims=True) + EPS) * g


def _sigmoid(x):
    return 0.5 * jnp.tanh(0.5 * x) + 0.5


def _silu(x):
    h = 0.5 * x
    return h + h * jnp.tanh(h)


def _gelu_tanh(x):
    return 0.5 * x * (1.0 + jnp.tanh(math.sqrt(2.0 / math.pi) * (x + 0.044715 * (x * x * x))))


def _norm_proj_kernel(x_ref, g_ref, w_ref, o_ref, ut_ref, h_ref, p_ref):
    j = pl.program_id(1)
    rows = BATCH * PROJ_TL

    @pl.when(j == 0)
    def _():
        h_ref[...] = _rms(x_ref[...].reshape(rows, D_MODEL), g_ref[...]).astype(BF16)

    u_lo, u_hi = COL_U // TN, COL_GATE_R // TN - 1

    def project():
        acc = jnp.dot(h_ref[...], w_ref[...], preferred_element_type=F32)
        o_ref[...] = acc.astype(o_ref.dtype).reshape(BATCH, PROJ_TL, TN)
        return acc

    def stash(acc, slot):
        for k in range(TN // LANES):
            for b in range(BATCH):
                p_ref[slot, k, pl.ds(b, PROJ_TL, stride=BATCH), :] = (
                    acc[b * PROJ_TL:(b + 1) * PROJ_TL, k * LANES:(k + 1) * LANES])

    def relayout(slot):
        blk = _lane_block_ids()
        for k in range(TN // LANES):
            for pair in range(PROJ_TL // SSM_T // 2):
                halves = []
                for c in (2 * pair, 2 * pair + 1):
                    parts = []
                    for s_hi in range(SSM_T // BLOCKS_PER_VREG):
                        v = [p_ref[slot, k, _token_rows(c, s_hi * BLOCKS_PER_VREG + s_lo), :]
                             for s_lo in range(BLOCKS_PER_VREG)]
                        parts.append(_block_transpose(v, blk))
                    halves.append([jnp.concatenate([p[gl] for p in parts], axis=1)
                                   for gl in range(GROUPS_PER_VREG)])
                for gl in range(GROUPS_PER_VREG):
                    ut_ref[k * GROUPS_PER_VREG + gl, pair * 2 * BATCH:(pair + 1) * 2 * BATCH, :] = (
                        jnp.concatenate([halves[0][gl], halves[1][gl]], axis=0).astype(BF16))

    relayout_step = (j > u_lo) & (j <= u_hi + 1)

    @pl.when(jnp.logical_not(relayout_step))
    def _():
        acc = project()

        @pl.when(j == u_lo)
        def _():
            stash(acc, 0)

    @pl.when(relayout_step)
    def _():
        acc = project()
        relayout((j - u_lo - 1) % 2)

        @pl.when(j <= u_hi)
        def _():
            stash(acc, (j - u_lo) % 2)


def _norm_proj(x3, gain, w, layer):
    n = w.shape[-1]
    u_lo, u_hi = COL_U // TN + 1, COL_GATE_R // TN
    rows = BATCH * PROJ_TL
    return pl.pallas_call(
        _norm_proj_kernel,
        grid=(SEQ // PROJ_TL, n // TN),
        in_specs=[pl.BlockSpec((BATCH, PROJ_TL, D_MODEL), lambda i, j: (0, i, 0)),
                  pl.BlockSpec((None, 1, D_MODEL), lambda i, j: (layer, 0, 0)),
                  pl.BlockSpec((None, D_MODEL, TN), lambda i, j: (layer, 0, j))],
        out_specs=[pl.BlockSpec((BATCH, PROJ_TL, TN), lambda i, j: (0, i, j)),
                   pl.BlockSpec((TN // SSM_GROUP, rows // SSM_T, SSM_LANES),
                                lambda i, j: (jnp.clip(j, u_lo, u_hi) - u_lo, i, 0))],
        out_shape=[jax.ShapeDtypeStruct((BATCH, SEQ, n), BF16),
                   jax.ShapeDtypeStruct((SSM_GROUPS, SSM_ROWS, SSM_LANES), BF16)],
        scratch_shapes=[pltpu.VMEM((rows, D_MODEL), BF16),
                        pltpu.VMEM((2, TN // LANES, rows, LANES), F32)],
        compiler_params=_params(("parallel", "arbitrary")),
        name="norm_in_proj",
    )(x3, gain, w)


def _retention_kernel(lg_ref, q_ref, k_ref, v_ref, g_ref, gr_ref, cos_ref, sin_ref, o_ref,
                      qs_ref, ks_ref, ys_ref, sf_ref, sb_ref, *, layer):
    c_len = RET_CHUNK
    n_chunks = SEQ // c_len
    half = RET_QK_DIM // 2
    head = pl.program_id(1)
    lg_f = lg_ref[layer, 0, head]
    lg_b = lg_ref[layer, 1, head]

    cos = cos_ref[...]
    sin = sin_ref[...]
    q = q_ref[...].astype(F32)
    k = k_ref[...].astype(F32) * (RET_QK_DIM ** -0.5)
    qs_ref[:, :half] = q[:, :half] * cos - q[:, half:] * sin
    qs_ref[:, half:] = q[:, :half] * sin + q[:, half:] * cos
    ks_ref[:, :half] = k[:, :half] * cos - k[:, half:] * sin
    ks_ref[:, half:] = k[:, :half] * sin + k[:, half:] * cos

    t_row = lax.broadcasted_iota(jnp.int32, (c_len, RET_QK_DIM), 0).astype(F32)
    q_scale_f = jnp.exp(lg_f * (t_row + 1.0))
    q_scale_b = jnp.exp(lg_b * (c_len - t_row))
    k_scale_f = jnp.exp(lg_f * (c_len - 1.0 - t_row))
    k_scale_b = jnp.exp(lg_b * t_row)
    diff = (lax.broadcasted_iota(jnp.int32, (c_len, c_len), 0)
            - lax.broadcasted_iota(jnp.int32, (c_len, c_len), 1)).astype(F32)
    dmat = jnp.exp(jnp.where(diff >= 0, lg_f * diff, -lg_b * diff))
    decay_f = jnp.exp(jnp.full((1, RET_V_DIM), lg_f * c_len, F32))
    decay_b = jnp.exp(jnp.full((1, RET_V_DIM), lg_b * c_len, F32))

    sf_ref[...] = jnp.zeros_like(sf_ref)
    sb_ref[...] = jnp.zeros_like(sb_ref)

    def rows(c):
        return pl.ds(pl.multiple_of(c * c_len, c_len), c_len)

    def forward(c, carry):
        r = rows(c)
        qc = qs_ref[r, :]
        kc = ks_ref[r, :]
        vc = v_ref[r, :]
        scores = lax.dot_general(qc.astype(BF16), kc.astype(BF16), (((1,), (1,)), ((), ())),
                                 preferred_element_type=F32) * dmat
        y = jnp.dot(scores.astype(BF16), vc, preferred_element_type=F32)
        y = y + jnp.dot((qc * q_scale_f).astype(BF16), sf_ref[...].astype(BF16),
                        preferred_element_type=F32)
        ys_ref[r, :] = y
        kv = lax.dot_general((kc * k_scale_f).astype(BF16), vc, (((0,), (0,)), ((), ())),
                             preferred_element_type=F32)
        sf_ref[...] = decay_f * sf_ref[...] + kv
        return carry

    lax.fori_loop(0, n_chunks, forward, 0)

    def backward(i, carry):
        c = n_chunks - 1 - i
        r = rows(c)
        qc = qs_ref[r, :]
        kc = ks_ref[r, :]
        vc = v_ref[r, :]
        y = ys_ref[r, :] + jnp.dot((qc * q_scale_b).astype(BF16), sb_ref[...].astype(BF16),
                                   preferred_element_type=F32)
        kv = lax.dot_general((kc * k_scale_b).astype(BF16), vc, (((0,), (0,)), ((), ())),
                             preferred_element_type=F32)
        sb_ref[...] = decay_b * sb_ref[...] + kv
        yn = y * lax.rsqrt(jnp.mean(y * y, axis=-1, keepdims=True) + EPS)
        g = g_ref[r, :].astype(F32)
        gr = gr_ref[r, :].astype(F32)
        o_ref[r, :] = (_sigmoid(gr) * (_silu(g) * yn)).astype(o_ref.dtype)
        return carry

    lax.fori_loop(0, n_chunks, backward, 0)


def _retention(proj3, log_gamma, cos, sin, layer):
    qk_blk = RET_QK_DIM
    v_blk = RET_V_DIM
    return pl.pallas_call(
        functools.partial(_retention_kernel, layer=layer),
        grid=(BATCH, RET_HEADS),
        in_specs=[
            pl.BlockSpec(memory_space=pltpu.SMEM),
            pl.BlockSpec((None, SEQ, qk_blk), lambda b, h: (b, 0, COL_Q // qk_blk + h)),
            pl.BlockSpec((None, SEQ, qk_blk), lambda b, h: (b, 0, COL_K // qk_blk + h)),
            pl.BlockSpec((None, SEQ, v_blk), lambda b, h: (b, 0, COL_V // v_blk + h)),
            pl.BlockSpec((None, SEQ, v_blk), lambda b, h: (b, 0, COL_G // v_blk + h)),
            pl.BlockSpec((None, SEQ, v_blk), lambda b, h: (b, 0, COL_GATE_R // v_blk + h)),
            pl.BlockSpec((SEQ, RET_QK_DIM // 2), lambda b, h: (0, 0)),
            pl.BlockSpec((SEQ, RET_QK_DIM // 2), lambda b, h: (0, 0)),
        ],
        out_specs=pl.BlockSpec((None, SEQ, v_blk), lambda b, h: (b, 0, h)),
        out_shape=jax.ShapeDtypeStruct((BATCH, SEQ, D_MODEL), BF16),
        scratch_shapes=[pltpu.VMEM((SEQ, RET_QK_DIM), F32),
                        pltpu.VMEM((SEQ, RET_QK_DIM), F32),
                        pltpu.VMEM((SEQ, RET_V_DIM), F32),
                        pltpu.VMEM((RET_QK_DIM, RET_V_DIM), F32),
                        pltpu.VMEM((RET_QK_DIM, RET_V_DIM), F32)],
        compiler_params=_params(("parallel", "arbitrary")),
        name="retention",
    )(log_gamma, proj3, proj3, proj3, proj3, proj3, cos, sin)


def _ssm_factors(a_re, a_im, log_dt, b_re, b_im, c_re, c_im):
    t_len, p = SSM_T, SSM_STATE
    depth = a_re.shape[0]
    ar_t = np.arange(t_len, dtype=np.float32)
    n_state = np.stack([t_len - 1.0 - ar_t, ar_t])
    n_out = np.stack([ar_t + 1.0, t_len - ar_t])
    dt = jnp.exp(log_dt)[..., None]
    lr, li = a_re, a_im
    lrdt, lidt = lr * dt, li * dt

    def powers(n):
        e = n[None, :, None, :, None]
        mag = jnp.exp(e * lrdt[:, :, :, None, :])
        ang = e * lidt[:, :, :, None, :]
        return mag * jnp.cos(ang), mag * jnp.sin(ang)

    mag1 = jnp.exp(lrdt)
    lbr, lbi = mag1 * jnp.cos(lidt), mag1 * jnp.sin(lidt)
    den = lr * lr + li * li
    nr, ni = lbr - 1.0, lbi
    fr, fi = (nr * lr + ni * li) / den, (ni * lr - nr * li) / den
    bt_r = jnp.swapaxes(b_re, -1, -2)
    bt_i = jnp.swapaxes(b_im, -1, -2)
    bbr = fr[..., None, :] * bt_r - fi[..., None, :] * bt_i
    bbi = fr[..., None, :] * bt_i + fi[..., None, :] * bt_r

    pr, pi = powers(n_state)
    qr, qi = powers(n_out)
    cat = lambda a, b: jnp.concatenate([a, b], axis=-1)
    fac = jnp.stack([cat(pr, pr), cat(pi, pi), cat(bbr, bbi), cat(-bbi, bbr),
                     cat(qr, qr), cat(qi, qi), cat(c_re, -c_im), cat(-c_im, -c_re)],
                    axis=3).astype(F32)

    mag_t = jnp.exp(t_len * lrdt)
    ar, ai = mag_t * jnp.cos(t_len * lidt), mag_t * jnp.sin(t_len * lidt)
    a1 = jnp.concatenate([ar, ar], axis=-1)
    a2 = jnp.concatenate([-ai, ai], axis=-1)
    coeffs = jnp.stack([a1, a2, -a2], axis=3)
    coeffs = coeffs.transpose(0, 2, 1, 3, 4).reshape(depth, SSM_GROUPS, 6, 2 * p)
    coeffs = jnp.broadcast_to(coeffs[:, :, :, None, :], (depth, SSM_GROUPS, 6, BATCH, 2 * p))
    return fac, coeffs.astype(F32)


def _expand_factors(f_ref, g):
    p1, p2, b1, b2, q1, q2, c1, c2 = [f_ref[g, n] for n in range(8)]
    s_cat = jnp.concatenate([p1[s:s + 1, :] * b1 + p2[s:s + 1, :] * b2 for s in range(SSM_T)], axis=0)
    i_cat = jnp.concatenate([q1[t:t + 1, :] * c1 + q2[t:t + 1, :] * c2 for t in range(SSM_T)], axis=0)
    c_tile = jnp.concatenate([c1] * SSM_T, axis=0)
    return s_cat, i_cat, c_tile


def _ssm_weights_kernel(ff_ref, fb_ref, wi_ref, ws_ref, wx_ref):
    nt = (((1,), (1,)), ((), ()))
    blk = lax.broadcasted_iota(jnp.int32, (SSM_LANES, LANES), 1) // SSM_GROUP
    for g in range(SSM_WGB):
        sf, i_f, cf = _expand_factors(ff_ref, g)
        sb, i_b, cb = _expand_factors(fb_ref, g)
        kf = lax.dot_general(sf, cf, nt, precision=lax.Precision.HIGHEST,
                             preferred_element_type=F32)
        kb = lax.dot_general(sb, cb, nt, precision=lax.Precision.HIGHEST,
                             preferred_element_type=F32)
        halves = []
        for half in range(SSM_LANES // LANES):
            lanes = slice(half * LANES, (half + 1) * LANES)
            acc = jnp.zeros((SSM_LANES, LANES), F32)
            for t_lo in range(BLOCKS_PER_VREG):
                t = half * BLOCKS_PER_VREG + t_lo
                up = (SSM_T - 1 - t) * SSM_GROUP
                down = t * SSM_GROUP
                parts_f = [kf[up:, lanes]] + ([jnp.zeros((up, LANES), F32)] if up else [])
                parts_b = ([jnp.zeros((down, LANES), F32)] if down else []) + [kb[:SSM_LANES - down, lanes]]
                shifted = jnp.concatenate(parts_f, axis=0) + jnp.concatenate(parts_b, axis=0)
                acc = jnp.where(blk == t_lo, shifted, acc)
            halves.append(acc)
        wi_ref[g] = jnp.concatenate(halves, axis=1).astype(wi_ref.dtype)
        ws_ref[g] = jnp.concatenate([sf, pltpu.roll(sf, SSM_STATE, axis=1),
                                     sb, pltpu.roll(sb, SSM_STATE, axis=1)], axis=1).astype(ws_ref.dtype)
        wx_ref[g] = jnp.concatenate([i_f, i_b], axis=1).astype(wx_ref.dtype)


def _ssm_weights(fac):
    depth = fac.shape[0]
    gb = SSM_WGB
    blk = (None, None, gb) + fac.shape[3:]
    fwd = pl.BlockSpec(blk, lambda l, i: (l, 0, i, 0, 0, 0))
    bwd = pl.BlockSpec(blk, lambda l, i: (l, 1, i, 0, 0, 0))

    def out(n):
        return (pl.BlockSpec((None, gb, SSM_LANES, n), lambda l, i: (l, i, 0, 0)),
                jax.ShapeDtypeStruct((depth, SSM_GROUPS, SSM_LANES, n), BF16))

    specs, shapes = zip(out(SSM_LANES), out(8 * SSM_STATE), out(4 * SSM_STATE))
    return pl.pallas_call(
        _ssm_weights_kernel,
        grid=(depth, SSM_GROUPS // gb),
        in_specs=[fwd, bwd],
        out_specs=list(specs),
        out_shape=list(shapes),
        compiler_params=_params(("parallel", "parallel")),
        name="ssm_weights",
    )(fac, fac)


def _lane_block_ids():
    return lax.broadcasted_iota(jnp.int32, (BATCH, LANES), 1) // SSM_GROUP


def _token_rows(c, t):
    first = (c * SSM_T + t) * BATCH
    return pl.ds(first if isinstance(first, int) else pl.multiple_of(first, BATCH), BATCH)


def _block_transpose(src, blk):
    n = BLOCKS_PER_VREG
    rolled = []
    for d in range(n):
        m = src[d % n]
        for a in range(1, n):
            m = jnp.where(blk == a, src[(a + d) % n], m)
        rolled.append(pltpu.roll(m, d * SSM_GROUP, axis=1) if d else m)
    out = []
    for a in range(n):
        t = rolled[(0 - a) % n]
        for b in range(1, n):
            t = jnp.where(blk == b, rolled[(b - a) % n], t)
        out.append(t)
    return out


def _ssm_unpack_kernel(y_ref, o_ref):
    blk = _lane_block_ids()

    def body(it, carry):
        for pair in range(PACK_CHUNKS // 2):
            first = (it * (PACK_CHUNKS // 2) + pair) * 2 * BATCH
            rows = pl.ds(pl.multiple_of(first, 2 * BATCH), 2 * BATCH)
            y = [y_ref[gl, rows, :].astype(F32) for gl in range(GROUPS_PER_VREG)]
            for k in range(2):
                c = it * PACK_CHUNKS + 2 * pair + k
                for t_hi in range(SSM_T // BLOCKS_PER_VREG):
                    src = [y[gl][k * BATCH:(k + 1) * BATCH, t_hi * LANES:(t_hi + 1) * LANES]
                           for gl in range(GROUPS_PER_VREG)]
                    out = _block_transpose(src, blk)
                    for t_lo in range(BLOCKS_PER_VREG):
                        o_ref[_token_rows(c, t_hi * BLOCKS_PER_VREG + t_lo), :] = out[t_lo]
        return carry

    lax.fori_loop(0, SSM_CHUNKS // PACK_CHUNKS, body, 0)


def _ssm_unpack(a_t):
    return pl.pallas_call(
        _ssm_unpack_kernel,
        grid=(D_MODEL // LANES,),
        in_specs=[pl.BlockSpec((GROUPS_PER_VREG, SSM_ROWS, SSM_LANES), lambda i: (i, 0, 0))],
        out_specs=pl.BlockSpec((None, TOKENS, LANES), lambda i: (i, 0, 0)),
        out_shape=jax.ShapeDtypeStruct((D_MODEL // LANES, TOKENS, LANES), F32),
        compiler_params=_params(("parallel",)),
        name="ssm_unpack",
    )(a_t)


def _ssm_kernel(u_ref, wi_ref, ws_ref, wx_ref, a_ref, d_ref, o_ref, s_ref, x_ref):
    p2 = 2 * SSM_STATE
    nt = (((1,), (1,)), ((), ()))
    for g in range(SSM_GB):
        s_ref[g] = jnp.dot(u_ref[g], ws_ref[g], preferred_element_type=F32)

    def step(c, carry):
        rf = pl.ds(pl.multiple_of(c * BATCH, BATCH), BATCH)
        rb = pl.ds(pl.multiple_of((SSM_CHUNKS - 1 - c) * BATCH, BATCH), BATCH)
        new = []
        for g in range(SSM_GB):
            xf, xfs, xb, xbs = carry[4 * g:4 * g + 4]
            x_ref[g, rf, 0:p2] = xf
            x_ref[g, rb, p2:2 * p2] = xb
            nxf = a_ref[g, 0] * xf + a_ref[g, 1] * xfs + s_ref[g, rf, 0:p2]
            nxfs = a_ref[g, 0] * xfs + a_ref[g, 2] * xf + s_ref[g, rf, p2:2 * p2]
            nxb = a_ref[g, 3] * xb + a_ref[g, 4] * xbs + s_ref[g, rb, 2 * p2:3 * p2]
            nxbs = a_ref[g, 3] * xbs + a_ref[g, 5] * xb + s_ref[g, rb, 3 * p2:4 * p2]
            new += [nxf, nxfs, nxb, nxbs]
        return tuple(new)

    zero = jnp.zeros((BATCH, p2), F32)
    lax.fori_loop(0, SSM_CHUNKS, step, (zero,) * (4 * SSM_GB))

    for g in range(SSM_GB):
        y = jnp.dot(u_ref[g], wi_ref[g], preferred_element_type=F32)
        y = y + lax.dot_general(x_ref[g].astype(BF16), wx_ref[g], nt, preferred_element_type=F32)
        o_ref[g] = _gelu_tanh(y + d_ref[g] * u_ref[g].astype(F32)).astype(o_ref.dtype)


def _ssm(u_t, w_intra, w_state, w_inter_t, coeffs, d_t, layer):
    gb = SSM_GB

    def wspec(n):
        return pl.BlockSpec((None, gb, SSM_LANES, n), lambda i: (layer, i, 0, 0))

    return pl.pallas_call(
        _ssm_kernel,
        grid=(SSM_GROUPS // gb,),
        in_specs=[pl.BlockSpec((gb, SSM_ROWS, SSM_LANES), lambda i: (i, 0, 0)),
                  wspec(SSM_LANES), wspec(8 * SSM_STATE), wspec(4 * SSM_STATE),
                  pl.BlockSpec((None, gb, 6, BATCH, 2 * SSM_STATE), lambda i: (layer, i, 0, 0, 0)),
                  pl.BlockSpec((None, gb, 1, SSM_LANES), lambda i: (layer, i, 0, 0))],
        out_specs=pl.BlockSpec((gb, SSM_ROWS, SSM_LANES), lambda i: (i, 0, 0)),
        out_shape=jax.ShapeDtypeStruct((SSM_GROUPS, SSM_ROWS, SSM_LANES), BF16),
        scratch_shapes=[pltpu.VMEM((gb, SSM_ROWS, 8 * SSM_STATE), F32),
                        pltpu.VMEM((gb, SSM_ROWS, 4 * SSM_STATE), F32)],
        compiler_params=_params(("parallel",)),
        name="ssm_core",
    )(u_t, w_intra, w_state, w_inter_t, coeffs, d_t)


def _glu_merge_kernel(a_ref, w_ref, b_ref, gs_ref, ret_ref, o_ref, af_ref, ab_ref):
    rows = BATCH * GLU_TL
    for k in range(D_MODEL // LANES):
        for b in range(BATCH):
            af_ref[b * GLU_TL:(b + 1) * GLU_TL, k * LANES:(k + 1) * LANES] = (
                a_ref[k, pl.ds(b, GLU_TL, stride=BATCH), :])
    ab_ref[...] = af_ref[...].astype(BF16)
    for n in range(D_MODEL // GLU_TN):
        cols = slice(n * GLU_TN, (n + 1) * GLU_TN)
        z = jnp.dot(ab_ref[...], w_ref[:, cols], preferred_element_type=F32) + b_ref[:, cols]
        ssm_out = af_ref[:, cols] * _sigmoid(z)
        gate = _sigmoid(gs_ref[:, :, cols].reshape(rows, GLU_TN).astype(F32))
        merged = ret_ref[:, :, cols].reshape(rows, GLU_TN).astype(F32) + gate * ssm_out
        o_ref[:, :, cols] = merged.astype(o_ref.dtype).reshape(BATCH, GLU_TL, GLU_TN)


def _glu_merge(a_lb, proj3, w_glu, b_glu, ret_g3, layer):
    rows = BATCH * GLU_TL
    tile = pl.BlockSpec((BATCH, GLU_TL, D_MODEL), lambda i: (0, i, 0))
    return pl.pallas_call(
        _glu_merge_kernel,
        grid=(SEQ // GLU_TL,),
        in_specs=[pl.BlockSpec((D_MODEL // LANES, rows, LANES), lambda i: (0, i, 0)),
                  pl.BlockSpec((None, D_MODEL, D_MODEL), lambda i: (layer, 0, 0),
                               pipeline_mode=pl.Buffered(1)),
                  pl.BlockSpec((None, 1, D_MODEL), lambda i: (layer, 0, 0)),
                  pl.BlockSpec((BATCH, GLU_TL, D_MODEL), lambda i: (0, i, COL_GATE_S // D_MODEL)),
                  tile],
        out_specs=tile,
        out_shape=jax.ShapeDtypeStruct((BATCH, SEQ, D_MODEL), BF16),
        scratch_shapes=[pltpu.VMEM((rows, D_MODEL), F32), pltpu.VMEM((rows, D_MODEL), BF16)],
        compiler_params=_params(("parallel",)),
        name="glu_merge",
    )(a_lb, w_glu, b_glu, proj3, ret_g3)


def _proj_residual_kernel(a_ref, w_ref, x_ref, o_ref):
    o_ref[...] = x_ref[...] + jnp.dot(a_ref[...], w_ref[...], preferred_element_type=F32)


def _proj_residual(a, w, x, layer):
    m, kdim = a.shape
    n = w.shape[-1]
    return pl.pallas_call(
        _proj_residual_kernel,
        grid=(m // TM, n // TN),
        in_specs=[pl.BlockSpec((TM, kdim), lambda i, j: (i, 0)),
                  pl.BlockSpec((None, kdim, TN), lambda i, j: (layer, 0, j)),
                  pl.BlockSpec((TM, TN), lambda i, j: (i, j))],
        out_specs=pl.BlockSpec((TM, TN), lambda i, j: (i, j)),
        out_shape=jax.ShapeDtypeStruct((m, n), F32),
        compiler_params=_params(("parallel", "parallel")),
        name="out_proj_residual",
    )(a, w, x)


def _ffn_kernel(x_ref, g_ref, wg_ref, wu_ref, wd_ref, gf_ref, o_ref, h_ref, *, final_norm):
    j = pl.program_id(1)

    @pl.when(j == 0)
    def _():
        x = x_ref[...]
        h_ref[...] = _rms(x, g_ref[...]).astype(BF16)
        o_ref[...] = x

    h = h_ref[...]
    gate = jnp.dot(h, wg_ref[...], preferred_element_type=F32)
    up = jnp.dot(h, wu_ref[...], preferred_element_type=F32)
    act = (_silu(gate) * up).astype(BF16)
    o_ref[...] += jnp.dot(act, wd_ref[...], preferred_element_type=F32)

    if final_norm:
        @pl.when(j == pl.num_programs(1) - 1)
        def _():
            o_ref[...] = _rms(o_ref[...], gf_ref[...])


def _ffn(x, gain, w_gate, w_up, w_down, final_gain, layer, final_norm):
    m, d = x.shape
    f = w_gate.shape[-1]
    return pl.pallas_call(
        functools.partial(_ffn_kernel, final_norm=final_norm),
        grid=(m // FFN_TM, f // FFN_TF),
        in_specs=[pl.BlockSpec((FFN_TM, d), lambda i, j: (i, 0)),
                  pl.BlockSpec((None, 1, d), lambda i, j: (layer, 0, 0)),
                  pl.BlockSpec((None, d, FFN_TF), lambda i, j: (layer, 0, j)),
                  pl.BlockSpec((None, d, FFN_TF), lambda i, j: (layer, 0, j)),
                  pl.BlockSpec((None, FFN_TF, d), lambda i, j: (layer, j, 0)),
                  pl.BlockSpec((1, d), lambda i, j: (0, 0))],
        out_specs=pl.BlockSpec((FFN_TM, d), lambda i, j: (i, 0)),
        out_shape=jax.ShapeDtypeStruct((m, d), F32),
        scratch_shapes=[pltpu.VMEM((FFN_TM, d), BF16)],
        compiler_params=_params(("parallel", "arbitrary")),
        name="ffn",
    )(x, gain, w_gate, w_up, w_down, final_gain)


def _rotary_tables():
    half = RET_QK_DIM // 2
    inv = 1.0 / (ROPE_BASE ** (jnp.arange(half, dtype=F32) / half))
    ang = jnp.arange(SEQ, dtype=F32)[:, None] * inv[None, :]
    return jnp.cos(ang), jnp.sin(ang)


def kernel(x, ln_mix_g, w_in, ret_log_gamma, ssm_a_re, ssm_a_im, ssm_log_dt, ssm_b_re, ssm_b_im,
           ssm_c_re, ssm_c_im, ssm_d, w_glu, b_glu, w_out, ln_ffn_g, w_ffn_gate, w_ffn_up,
           w_ffn_down, ln_final_g):
    depth = w_in.shape[0]
    cos, sin = _rotary_tables()
    row = lambda p: p.astype(F32).reshape(depth, 1, D_MODEL)
    mix_gain, ffn_gain, glu_bias = row(ln_mix_g), row(ln_ffn_g), row(b_glu)
    d_t = jnp.tile(ssm_d.astype(F32).reshape(depth, SSM_GROUPS, 1, SSM_GROUP), (1, 1, 1, SSM_T))
    final_gain = ln_final_g.astype(F32).reshape(1, D_MODEL)
    log_gamma = ret_log_gamma.astype(F32)
    w_in_b, w_glu_b, w_out_b = w_in.astype(BF16), w_glu.astype(BF16), w_out.astype(BF16)
    w_gate_b, w_up_b, w_down_b = (w_ffn_gate.astype(BF16), w_ffn_up.astype(BF16),
                                  w_ffn_down.astype(BF16))
    fac, coeffs = _ssm_factors(ssm_a_re, ssm_a_im, ssm_log_dt, ssm_b_re, ssm_b_im,
                               ssm_c_re, ssm_c_im)
    w_intra, w_state, w_inter_t = _ssm_weights(fac)

    xt = x.reshape(TOKENS, D_MODEL)
    for i in range(depth):
        proj3, u_t = _norm_proj(xt.reshape(BATCH, SEQ, D_MODEL), mix_gain, w_in_b, i)
        ret_g3 = _retention(proj3, log_gamma, cos, sin, i)
        a_lb = _ssm_unpack(_ssm(u_t, w_intra, w_state, w_inter_t, coeffs, d_t, i))
        merged3 = _glu_merge(a_lb, proj3, w_glu_b, glu_bias, ret_g3, i)
        xt = _proj_residual(merged3.reshape(TOKENS, D_MODEL), w_out_b, xt, i)
        xt = _ffn(xt, ffn_gain, w_gate_b, w_up_b, w_down_b, final_gain, i,
                  final_norm=(i == depth - 1))
    return xt.reshape(BATCH, SEQ, D_MODEL)
```

```python
import functools
import math

import numpy as np
import jax
import jax.numpy as jnp
from jax import lax
from jax.experimental import pallas as pl
from jax.experimental.pallas import tpu as pltpu

F32 = jnp.float32
BF16 = jnp.bfloat16

D_MODEL = 2048
BATCH = 8
SEQ = 2048
TOKENS = BATCH * SEQ
RET_HEADS = 4
RET_QK_DIM = 256
RET_V_DIM = D_MODEL // RET_HEADS
RET_QK_WIDTH = RET_HEADS * RET_QK_DIM
ROPE_BASE = 10000.0
SSM_GROUP = 16
SSM_GROUPS = D_MODEL // SSM_GROUP
SSM_STATE = 64
D_FF = ((8 * D_MODEL // 3 + 255) // 256) * 256
IN_WIDTH = 2 * RET_QK_WIDTH + 5 * D_MODEL
EPS = 1e-6

COL_Q = 0
COL_K = RET_QK_WIDTH
COL_V = 2 * RET_QK_WIDTH
COL_G = COL_V + D_MODEL
COL_U = COL_G + D_MODEL
COL_GATE_R = COL_U + D_MODEL
COL_GATE_S = COL_GATE_R + D_MODEL

LANES = 128
RET_CHUNK = 256
SSM_T = 16
SSM_CHUNKS = SEQ // SSM_T
SSM_ROWS = SSM_CHUNKS * BATCH
SSM_LANES = SSM_T * SSM_GROUP
SSM_GB = 4
SSM_WGB = 8
GROUPS_PER_VREG = LANES // SSM_GROUP
BLOCKS_PER_VREG = LANES // SSM_GROUP

VMEM_LIMIT = 56 * 1024 * 1024

TM = 1024
PROJ_TL = 128
TN = 1024
GLU_TL = 64
GLU_TN = 512
PACK_CHUNKS = 4
FFN_TM = 1024
FFN_TF = 512


def _params(sem):
    return pltpu.CompilerParams(dimension_semantics=sem, vmem_limit_bytes=VMEM_LIMIT)


def _rms(x, g):
    return x * lax.rsqrt(jnp.mean(x * x, axis=-1, keepdims=True) + EPS) * g


def _sigmoid(x):
    return 0.5 * jnp.tanh(0.5 * x) + 0.5


def _silu(x):
    h = 0.5 * x
    return h + h * jnp.tanh(h)


def _gelu_tanh(x):
    return 0.5 * x * (1.0 + jnp.tanh(math.sqrt(2.0 / math.pi) * (x + 0.044715 * (x * x * x))))


def _norm_proj_kernel(x_ref, g_ref, w_ref, o_ref, ut_ref, h_ref, p_ref):
    j = pl.program_id(1)
    rows = BATCH * PROJ_TL

    @pl.when(j == 0)
    def _():
        h_ref[...] = _rms(x_ref[...].reshape(rows, D_MODEL), g_ref[...]).astype(BF16)

    u_lo, u_hi = COL_U // TN, COL_GATE_R // TN - 1

    def project():
        acc = jnp.dot(h_ref[...], w_ref[...], preferred_element_type=F32)
        o_ref[...] = acc.astype(o_ref.dtype).reshape(BATCH, PROJ_TL, TN)
        return acc

    def stash(acc, slot):
        for k in range(TN // LANES):
            for b in range(BATCH):
                p_ref[slot, k, pl.ds(b, PROJ_TL, stride=BATCH), :] = (
                    acc[b * PROJ_TL:(b + 1) * PROJ_TL, k * LANES:(k + 1) * LANES])

    def relayout(slot):
        blk = _lane_block_ids()
        for k in range(TN // LANES):
            for pair in range(PROJ_TL // SSM_T // 2):
                halves = []
                for c in (2 * pair, 2 * pair + 1):
                    parts = []
                    for s_hi in range(SSM_T // BLOCKS_PER_VREG):
                        v = [p_ref[slot, k, _token_rows(c, s_hi * BLOCKS_PER_VREG + s_lo), :]
                             for s_lo in range(BLOCKS_PER_VREG)]
                        parts.append(_block_transpose(v, blk))
                    halves.append([jnp.concatenate([p[gl] for p in parts], axis=1)
                                   for gl in range(GROUPS_PER_VREG)])
                for gl in range(GROUPS_PER_VREG):
                    ut_ref[k * GROUPS_PER_VREG + gl, pair * 2 * BATCH:(pair + 1) * 2 * BATCH, :] = (
                        jnp.concatenate([halves[0][gl], halves[1][gl]], axis=0).astype(BF16))

    relayout_step = (j > u_lo) & (j <= u_hi + 1)

    @pl.when(jnp.logical_not(relayout_step))
    def _():
        acc = project()

        @pl.when(j == u_lo)
        def _():
            stash(acc, 0)

    @pl.when(relayout_step)
    def _():
        acc = project()
        relayout((j - u_lo - 1) % 2)

        @pl.when(j <= u_hi)
        def _():
            stash(acc, (j - u_lo) % 2)


def _norm_proj(x3, gain, w, layer):
    n = w.shape[1] * TN
    u_lo, u_hi = COL_U // TN + 1, COL_GATE_R // TN
    rows = BATCH * PROJ_TL
    return pl.pallas_call(
        _norm_proj_kernel,
        grid=(SEQ // PROJ_TL, n // TN),
        in_specs=[pl.BlockSpec((BATCH, PROJ_TL, D_MODEL), lambda i, j: (0, i, 0)),
                  pl.BlockSpec((None, 1, D_MODEL), lambda i, j: (layer, 0, 0)),
                  pl.BlockSpec((None, None, D_MODEL, TN), lambda i, j: (layer, j, 0, 0))],
        out_specs=[pl.BlockSpec((BATCH, PROJ_TL, TN), lambda i, j: (0, i, j)),
                   pl.BlockSpec((TN // SSM_GROUP, rows // SSM_T, SSM_LANES),
                                lambda i, j: (jnp.clip(j, u_lo, u_hi) - u_lo, i, 0))],
        out_shape=[jax.ShapeDtypeStruct((BATCH, SEQ, n), BF16),
                   jax.ShapeDtypeStruct((SSM_GROUPS, SSM_ROWS, SSM_LANES), BF16)],
        scratch_shapes=[pltpu.VMEM((rows, D_MODEL), BF16),
                        pltpu.VMEM((2, TN // LANES, rows, LANES), F32)],
        compiler_params=_params(("parallel", "arbitrary")),
        name="norm_in_proj",
    )(x3, gain, w)


def _retention_kernel(lg_ref, q_ref, k_ref, v_ref, g_ref, gr_ref, cos_ref, sin_ref, o_ref,
                      qs_ref, ks_ref, ys_ref, sf_ref, sb_ref, *, layer):
    c_len = RET_CHUNK
    n_chunks = SEQ // c_len
    half = RET_QK_DIM // 2
    head = pl.program_id(1)
    lg_f = lg_ref[layer, 0, head]
    lg_b = lg_ref[layer, 1, head]

    cos = cos_ref[...]
    sin = sin_ref[...]
    q = q_ref[...].astype(F32)
    k = k_ref[...].astype(F32) * (RET_QK_DIM ** -0.5)
    qs_ref[:, :half] = q[:, :half] * cos - q[:, half:] * sin
    qs_ref[:, half:] = q[:, :half] * sin + q[:, half:] * cos
    ks_ref[:, :half] = k[:, :half] * cos - k[:, half:] * sin
    ks_ref[:, half:] = k[:, :half] * sin + k[:, half:] * cos

    t_row = lax.broadcasted_iota(jnp.int32, (c_len, RET_QK_DIM), 0).astype(F32)
    q_scale_f = jnp.exp(lg_f * (t_row + 1.0))
    q_scale_b = jnp.exp(lg_b * (c_len - t_row))
    k_scale_f = jnp.exp(lg_f * (c_len - 1.0 - t_row))
    k_scale_b = jnp.exp(lg_b * t_row)
    diff = (lax.broadcasted_iota(jnp.int32, (c_len, c_len), 0)
            - lax.broadcasted_iota(jnp.int32, (c_len, c_len), 1)).astype(F32)
    dmat = jnp.exp(jnp.where(diff >= 0, lg_f * diff, -lg_b * diff))
    decay_f = jnp.exp(jnp.full((1, RET_V_DIM), lg_f * c_len, F32))
    decay_b = jnp.exp(jnp.full((1, RET_V_DIM), lg_b * c_len, F32))

    sf_ref[...] = jnp.zeros_like(sf_ref)
    sb_ref[...] = jnp.zeros_like(sb_ref)

    def rows(c):
        return pl.ds(pl.multiple_of(c * c_len, c_len), c_len)

    def forward(c, carry):
        r = rows(c)
        qc = qs_ref[r, :]
        kc = ks_ref[r, :]
        vc = v_ref[r, :]
        scores = lax.dot_general(qc.astype(BF16), kc.astype(BF16), (((1,), (1,)), ((), ())),
                                 preferred_element_type=F32) * dmat
        y = jnp.dot(scores.astype(BF16), vc, preferred_element_type=F32)
        y = y + jnp.dot((qc * q_scale_f).astype(BF16), sf_ref[...].astype(BF16),
                        preferred_element_type=F32)
        ys_ref[r, :] = y
        kv = lax.dot_general((kc * k_scale_f).astype(BF16), vc, (((0,), (0,)), ((), ())),
                             preferred_element_type=F32)
        sf_ref[...] = decay_f * sf_ref[...] + kv
        return carry

    lax.fori_loop(0, n_chunks, forward, 0)

    def backward(i, carry):
        c = n_chunks - 1 - i
        r = rows(c)
        qc = qs_ref[r, :]
        kc = ks_ref[r, :]
        vc = v_ref[r, :]
        y = ys_ref[r, :] + jnp.dot((qc * q_scale_b).astype(BF16), sb_ref[...].astype(BF16),
                                   preferred_element_type=F32)
        kv = lax.dot_general((kc * k_scale_b).astype(BF16), vc, (((0,), (0,)), ((), ())),
                             preferred_element_type=F32)
        sb_ref[...] = decay_b * sb_ref[...] + kv
        yn = y * lax.rsqrt(jnp.mean(y * y, axis=-1, keepdims=True) + EPS)
        g = g_ref[r, :].astype(F32)
        gr = gr_ref[r, :].astype(F32)
        o_ref[r, :] = (_sigmoid(gr) * (_silu(g) * yn)).astype(o_ref.dtype)
        return carry

    lax.fori_loop(0, n_chunks, backward, 0)


def _retention(proj3, log_gamma, cos, sin, layer):
    qk_blk = RET_QK_DIM
    v_blk = RET_V_DIM
    return pl.pallas_call(
        functools.partial(_retention_kernel, layer=layer),
        grid=(BATCH, RET_HEADS),
        in_specs=[
            pl.BlockSpec(memory_space=pltpu.SMEM),
            pl.BlockSpec((None, SEQ, qk_blk), lambda b, h: (b, 0, COL_Q // qk_blk + h)),
            pl.BlockSpec((None, SEQ, qk_blk), lambda b, h: (b, 0, COL_K // qk_blk + h)),
            pl.BlockSpec((None, SEQ, v_blk), lambda b, h: (b, 0, COL_V // v_blk + h)),
            pl.BlockSpec((None, SEQ, v_blk), lambda b, h: (b, 0, COL_G // v_blk + h)),
            pl.BlockSpec((None, SEQ, v_blk), lambda b, h: (b, 0, COL_GATE_R // v_blk + h)),
            pl.BlockSpec((SEQ, RET_QK_DIM // 2), lambda b, h: (0, 0)),
            pl.BlockSpec((SEQ, RET_QK_DIM // 2), lambda b, h: (0, 0)),
        ],
        out_specs=pl.BlockSpec((None, SEQ, v_blk), lambda b, h: (b, 0, h)),
        out_shape=jax.ShapeDtypeStruct((BATCH, SEQ, D_MODEL), BF16),
        scratch_shapes=[pltpu.VMEM((SEQ, RET_QK_DIM), F32),
                        pltpu.VMEM((SEQ, RET_QK_DIM), F32),
                        pltpu.VMEM((SEQ, RET_V_DIM), F32),
                        pltpu.VMEM((RET_QK_DIM, RET_V_DIM), F32),
                        pltpu.VMEM((RET_QK_DIM, RET_V_DIM), F32)],
        compiler_params=_params(("parallel", "arbitrary")),
        name="retention",
    )(log_gamma, proj3, proj3, proj3, proj3, proj3, cos, sin)


def _ssm_factors(a_re, a_im, log_dt, b_re, b_im, c_re, c_im):
    t_len, p = SSM_T, SSM_STATE
    depth = a_re.shape[0]
    ar_t = np.arange(t_len, dtype=np.float32)
    n_state = np.stack([t_len - 1.0 - ar_t, ar_t])
    n_out = np.stack([ar_t + 1.0, t_len - ar_t])
    dt = jnp.exp(log_dt)[..., None]
    lr, li = a_re, a_im
    lrdt, lidt = lr * dt, li * dt

    def powers(n):
        e = n[None, :, None, :, None]
        mag = jnp.exp(e * lrdt[:, :, :, None, :])
        ang = e * lidt[:, :, :, None, :]
        return mag * jnp.cos(ang), mag * jnp.sin(ang)

    mag1 = jnp.exp(lrdt)
    lbr, lbi = mag1 * jnp.cos(lidt), mag1 * jnp.sin(lidt)
    den = lr * lr + li * li
    nr, ni = lbr - 1.0, lbi
    fr, fi = (nr * lr + ni * li) / den, (ni * lr - nr * li) / den
    bt_r = jnp.swapaxes(b_re, -1, -2)
    bt_i = jnp.swapaxes(b_im, -1, -2)
    bbr = fr[..., None, :] * bt_r - fi[..., None, :] * bt_i
    bbi = fr[..., None, :] * bt_i + fi[..., None, :] * bt_r

    pr, pi = powers(n_state)
    qr, qi = powers(n_out)
    cat = lambda a, b: jnp.concatenate([a, b], axis=-1)
    fac = jnp.stack([cat(pr, pr), cat(pi, pi), cat(bbr, bbi), cat(-bbi, bbr),
                     cat(qr, qr), cat(qi, qi), cat(c_re, -c_im), cat(-c_im, -c_re)],
                    axis=3).astype(F32)

    mag_t = jnp.exp(t_len * lrdt)
    ar, ai = mag_t * jnp.cos(t_len * lidt), mag_t * jnp.sin(t_len * lidt)
    a1 = jnp.concatenate([ar, ar], axis=-1)
    a2 = jnp.concatenate([-ai, ai], axis=-1)
    coeffs = jnp.stack([a1, a2, -a2], axis=3)
    coeffs = coeffs.transpose(0, 2, 1, 3, 4).reshape(depth, SSM_GROUPS, 6, 2 * p)
    coeffs = jnp.broadcast_to(coeffs[:, :, :, None, :], (depth, SSM_GROUPS, 6, BATCH, 2 * p))
    return fac, coeffs.astype(F32)


def _expand_factors(f_ref, g):
    p1, p2, b1, b2, q1, q2, c1, c2 = [f_ref[g, n] for n in range(8)]
    s_cat = jnp.concatenate([p1[s:s + 1, :] * b1 + p2[s:s + 1, :] * b2 for s in range(SSM_T)], axis=0)
    i_cat = jnp.concatenate([q1[t:t + 1, :] * c1 + q2[t:t + 1, :] * c2 for t in range(SSM_T)], axis=0)
    c_tile = jnp.concatenate([c1] * SSM_T, axis=0)
    return s_cat, i_cat, c_tile


def _ssm_weights_kernel(ff_ref, fb_ref, wi_ref, ws_ref, wx_ref):
    nt = (((1,), (1,)), ((), ()))
    blk = lax.broadcasted_iota(jnp.int32, (SSM_LANES, LANES), 1) // SSM_GROUP
    for g in range(SSM_WGB):
        sf, i_f, cf = _expand_factors(ff_ref, g)
        sb, i_b, cb = _expand_factors(fb_ref, g)
        kf = lax.dot_general(sf, cf, nt, precision=lax.Precision.HIGHEST,
                             preferred_element_type=F32)
        kb = lax.dot_general(sb, cb, nt, precision=lax.Precision.HIGHEST,
                             preferred_element_type=F32)
        halves = []
        for half in range(SSM_LANES // LANES):
            lanes = slice(half * LANES, (half + 1) * LANES)
            acc = jnp.zeros((SSM_LANES, LANES), F32)
            for t_lo in range(BLOCKS_PER_VREG):
                t = half * BLOCKS_PER_VREG + t_lo
                up = (SSM_T - 1 - t) * SSM_GROUP
                down = t * SSM_GROUP
                parts_f = [kf[up:, lanes]] + ([jnp.zeros((up, LANES), F32)] if up else [])
                parts_b = ([jnp.zeros((down, LANES), F32)] if down else []) + [kb[:SSM_LANES - down, lanes]]
                shifted = jnp.concatenate(parts_f, axis=0) + jnp.concatenate(parts_b, axis=0)
                acc = jnp.where(blk == t_lo, shifted, acc)
            halves.append(acc)
        wi_ref[g] = jnp.concatenate(halves, axis=1).astype(wi_ref.dtype)
        ws_ref[g] = jnp.concatenate([sf, pltpu.roll(sf, SSM_STATE, axis=1),
                                     sb, pltpu.roll(sb, SSM_STATE, axis=1)], axis=1).astype(ws_ref.dtype)
        wx_ref[g] = jnp.concatenate([i_f, i_b], axis=1).astype(wx_ref.dtype)


def _ssm_weights(fac):
    depth = fac.shape[0]
    gb = SSM_WGB
    blk = (None, None, gb) + fac.shape[3:]
    fwd = pl.BlockSpec(blk, lambda l, i: (l, 0, i, 0, 0, 0))
    bwd = pl.BlockSpec(blk, lambda l, i: (l, 1, i, 0, 0, 0))

    def out(n):
        return (pl.BlockSpec((None, gb, SSM_LANES, n), lambda l, i: (l, i, 0, 0)),
                jax.ShapeDtypeStruct((depth, SSM_GROUPS, SSM_LANES, n), BF16))

    specs, shapes = zip(out(SSM_LANES), out(8 * SSM_STATE), out(4 * SSM_STATE))
    return pl.pallas_call(
        _ssm_weights_kernel,
        grid=(depth, SSM_GROUPS // gb),
        in_specs=[fwd, bwd],
        out_specs=list(specs),
        out_shape=list(shapes),
        compiler_params=_params(("parallel", "parallel")),
        name="ssm_weights",
    )(fac, fac)


def _lane_block_ids():
    return lax.broadcasted_iota(jnp.int32, (BATCH, LANES), 1) // SSM_GROUP


def _token_rows(c, t):
    first = (c * SSM_T + t) * BATCH
    return pl.ds(first if isinstance(first, int) else pl.multiple_of(first, BATCH), BATCH)


def _block_transpose(src, blk):
    n = BLOCKS_PER_VREG
    rolled = []
    for d in range(n):
        m = src[d % n]
        for a in range(1, n):
            m = jnp.where(blk == a, src[(a + d) % n], m)
        rolled.append(pltpu.roll(m, d * SSM_GROUP, axis=1) if d else m)
    out = []
    for a in range(n):
        t = rolled[(0 - a) % n]
        for b in range(1, n):
            t = jnp.where(blk == b, rolled[(b - a) % n], t)
        out.append(t)
    return out


def _ssm_unpack_kernel(y_ref, o_ref):
    blk = _lane_block_ids()

    def body(it, carry):
        for pair in range(PACK_CHUNKS // 2):
            first = (it * (PACK_CHUNKS // 2) + pair) * 2 * BATCH
            rows = pl.ds(pl.multiple_of(first, 2 * BATCH), 2 * BATCH)
            y = [y_ref[gl, rows, :].astype(F32) for gl in range(GROUPS_PER_VREG)]
            for k in range(2):
                c = it * PACK_CHUNKS + 2 * pair + k
                for t_hi in range(SSM_T // BLOCKS_PER_VREG):
                    src = [y[gl][k * BATCH:(k + 1) * BATCH, t_hi * LANES:(t_hi + 1) * LANES]
                           for gl in range(GROUPS_PER_VREG)]
                    out = _block_transpose(src, blk)
                    for t_lo in range(BLOCKS_PER_VREG):
                        o_ref[_token_rows(c, t_hi * BLOCKS_PER_VREG + t_lo), :] = out[t_lo]
        return carry

    lax.fori_loop(0, SSM_CHUNKS // PACK_CHUNKS, body, 0)


def _ssm_unpack(a_t):
    return pl.pallas_call(
        _ssm_unpack_kernel,
        grid=(D_MODEL // LANES,),
        in_specs=[pl.BlockSpec((GROUPS_PER_VREG, SSM_ROWS, SSM_LANES), lambda i: (i, 0, 0))],
        out_specs=pl.BlockSpec((None, TOKENS, LANES), lambda i: (i, 0, 0)),
        out_shape=jax.ShapeDtypeStruct((D_MODEL // LANES, TOKENS, LANES), F32),
        compiler_params=_params(("parallel",)),
        name="ssm_unpack",
    )(a_t)


def _ssm_kernel(u_ref, wi_ref, ws_ref, wx_ref, a_ref, d_ref, o_ref, s_ref, x_ref):
    p2 = 2 * SSM_STATE
    nt = (((1,), (1,)), ((), ()))
    for g in range(SSM_GB):
        s_ref[g] = jnp.dot(u_ref[g], ws_ref[g], preferred_element_type=F32)

    def step(c, carry):
        rf = pl.ds(pl.multiple_of(c * BATCH, BATCH), BATCH)
        rb = pl.ds(pl.multiple_of((SSM_CHUNKS - 1 - c) * BATCH, BATCH), BATCH)
        new = []
        for g in range(SSM_GB):
            xf, xfs, xb, xbs = carry[4 * g:4 * g + 4]
            x_ref[g, rf, 0:p2] = xf
            x_ref[g, rb, p2:2 * p2] = xb
            nxf = a_ref[g, 0] * xf + a_ref[g, 1] * xfs + s_ref[g, rf, 0:p2]
            nxfs = a_ref[g, 0] * xfs + a_ref[g, 2] * xf + s_ref[g, rf, p2:2 * p2]
            nxb = a_ref[g, 3] * xb + a_ref[g, 4] * xbs + s_ref[g, rb, 2 * p2:3 * p2]
            nxbs = a_ref[g, 3] * xbs + a_ref[g, 5] * xb + s_ref[g, rb, 3 * p2:4 * p2]
            new += [nxf, nxfs, nxb, nxbs]
        return tuple(new)

    zero = jnp.zeros((BATCH, p2), F32)
    lax.fori_loop(0, SSM_CHUNKS, step, (zero,) * (4 * SSM_GB))

    for g in range(SSM_GB):
        y = jnp.dot(u_ref[g], wi_ref[g], preferred_element_type=F32)
        y = y + lax.dot_general(x_ref[g].astype(BF16), wx_ref[g], nt, preferred_element_type=F32)
        o_ref[g] = _gelu_tanh(y + d_ref[g] * u_ref[g].astype(F32)).astype(o_ref.dtype)


def _ssm(u_t, w_intra, w_state, w_inter_t, coeffs, d_t, layer):
    gb = SSM_GB

    def wspec(n):
        return pl.BlockSpec((None, gb, SSM_LANES, n), lambda i: (layer, i, 0, 0))

    return pl.pallas_call(
        _ssm_kernel,
        grid=(SSM_GROUPS // gb,),
        in_specs=[pl.BlockSpec((gb, SSM_ROWS, SSM_LANES), lambda i: (i, 0, 0)),
                  wspec(SSM_LANES), wspec(8 * SSM_STATE), wspec(4 * SSM_STATE),
                  pl.BlockSpec((None, gb, 6, BATCH, 2 * SSM_STATE), lambda i: (layer, i, 0, 0, 0)),
                  pl.BlockSpec((None, gb, 1, SSM_LANES), lambda i: (layer, i, 0, 0))],
        out_specs=pl.BlockSpec((gb, SSM_ROWS, SSM_LANES), lambda i: (i, 0, 0)),
        out_shape=jax.ShapeDtypeStruct((SSM_GROUPS, SSM_ROWS, SSM_LANES), BF16),
        scratch_shapes=[pltpu.VMEM((gb, SSM_ROWS, 8 * SSM_STATE), F32),
                        pltpu.VMEM((gb, SSM_ROWS, 4 * SSM_STATE), F32)],
        compiler_params=_params(("parallel",)),
        name="ssm_core",
    )(u_t, w_intra, w_state, w_inter_t, coeffs, d_t)


def _glu_merge_kernel(a_ref, w_ref, b_ref, gs_ref, ret_ref, o_ref, af_ref, ab_ref):
    rows = BATCH * GLU_TL
    for k in range(D_MODEL // LANES):
        for b in range(BATCH):
            af_ref[b * GLU_TL:(b + 1) * GLU_TL, k * LANES:(k + 1) * LANES] = (
                a_ref[k, pl.ds(b, GLU_TL, stride=BATCH), :])
    ab_ref[...] = af_ref[...].astype(BF16)
    for n in range(D_MODEL // GLU_TN):
        cols = slice(n * GLU_TN, (n + 1) * GLU_TN)
        z = jnp.dot(ab_ref[...], w_ref[:, cols], preferred_element_type=F32) + b_ref[:, cols]
        ssm_out = af_ref[:, cols] * _sigmoid(z)
        gate = _sigmoid(gs_ref[:, :, cols].reshape(rows, GLU_TN).astype(F32))
        merged = ret_ref[:, :, cols].reshape(rows, GLU_TN).astype(F32) + gate * ssm_out
        o_ref[:, :, cols] = merged.astype(o_ref.dtype).reshape(BATCH, GLU_TL, GLU_TN)


def _glu_merge(a_lb, proj3, w_glu, b_glu, ret_g3, layer):
    rows = BATCH * GLU_TL
    tile = pl.BlockSpec((BATCH, GLU_TL, D_MODEL), lambda i: (0, i, 0))
    return pl.pallas_call(
        _glu_merge_kernel,
        grid=(SEQ // GLU_TL,),
        in_specs=[pl.BlockSpec((D_MODEL // LANES, rows, LANES), lambda i: (0, i, 0)),
                  pl.BlockSpec((None, D_MODEL, D_MODEL), lambda i: (layer, 0, 0),
                               pipeline_mode=pl.Buffered(1)),
                  pl.BlockSpec((None, 1, D_MODEL), lambda i: (layer, 0, 0)),
                  pl.BlockSpec((BATCH, GLU_TL, D_MODEL), lambda i: (0, i, COL_GATE_S // D_MODEL)),
                  tile],
        out_specs=tile,
        out_shape=jax.ShapeDtypeStruct((BATCH, SEQ, D_MODEL), BF16),
        scratch_shapes=[pltpu.VMEM((rows, D_MODEL), F32), pltpu.VMEM((rows, D_MODEL), BF16)],
        compiler_params=_params(("parallel",)),
        name="glu_merge",
    )(a_lb, w_glu, b_glu, proj3, ret_g3)


def _proj_residual_kernel(a_ref, w_ref, x_ref, o_ref):
    o_ref[...] = x_ref[...] + jnp.dot(a_ref[...], w_ref[...], preferred_element_type=F32)


def _proj_residual(a, w, x, layer):
    m, kdim = a.shape
    n = w.shape[1] * TN
    return pl.pallas_call(
        _proj_residual_kernel,
        grid=(m // TM, n // TN),
        in_specs=[pl.BlockSpec((TM, kdim), lambda i, j: (i, 0)),
                  pl.BlockSpec((None, None, kdim, TN), lambda i, j: (layer, j, 0, 0)),
                  pl.BlockSpec((TM, TN), lambda i, j: (i, j))],
        out_specs=pl.BlockSpec((TM, TN), lambda i, j: (i, j)),
        out_shape=jax.ShapeDtypeStruct((m, n), F32),
        compiler_params=_params(("parallel", "parallel")),
        name="out_proj_residual",
    )(a, w, x)


def _ffn_kernel(x_ref, g_ref, wg_ref, wu_ref, wd_ref, gf_ref, o_ref, h_ref, *, final_norm):
    j = pl.program_id(1)

    @pl.when(j == 0)
    def _():
        x = x_ref[...]
        h_ref[...] = _rms(x, g_ref[...]).astype(BF16)
        o_ref[...] = x

    h = h_ref[...]
    gate = jnp.dot(h, wg_ref[...], preferred_element_type=F32)
    up = jnp.dot(h, wu_ref[...], preferred_element_type=F32)
    act = (_silu(gate) * up).astype(BF16)
    o_ref[...] += jnp.dot(act, wd_ref[...], preferred_element_type=F32)

    if final_norm:
        @pl.when(j == pl.num_programs(1) - 1)
        def _():
            o_ref[...] = _rms(o_ref[...], gf_ref[...])


def _ffn(x, gain, w_gate, w_up, w_down, final_gain, layer, final_norm):
    m, d = x.shape
    f = w_down.shape[1]
    return pl.pallas_call(
        functools.partial(_ffn_kernel, final_norm=final_norm),
        grid=(m // FFN_TM, f // FFN_TF),
        in_specs=[pl.BlockSpec((FFN_TM, d), lambda i, j: (i, 0)),
                  pl.BlockSpec((None, 1, d), lambda i, j: (layer, 0, 0)),
                  pl.BlockSpec((None, None, d, FFN_TF), lambda i, j: (layer, j, 0, 0)),
                  pl.BlockSpec((None, None, d, FFN_TF), lambda i, j: (layer, j, 0, 0)),
                  pl.BlockSpec((None, FFN_TF, d), lambda i, j: (layer, j, 0)),
                  pl.BlockSpec((1, d), lambda i, j: (0, 0))],
        out_specs=pl.BlockSpec((FFN_TM, d), lambda i, j: (i, 0)),
        out_shape=jax.ShapeDtypeStruct((m, d), F32),
        scratch_shapes=[pltpu.VMEM((FFN_TM, d), BF16)],
        compiler_params=_params(("parallel", "arbitrary")),
        name="ffn",
    )(x, gain, w_gate, w_up, w_down, final_gain)


def _column_blocks(w, tn):
    depth, kdim, n = w.shape
    return w.astype(BF16).reshape(depth, kdim, n // tn, tn).transpose(0, 2, 1, 3)


def _rotary_tables():
    half = RET_QK_DIM // 2
    inv = 1.0 / (ROPE_BASE ** (jnp.arange(half, dtype=F32) / half))
    ang = jnp.arange(SEQ, dtype=F32)[:, None] * inv[None, :]
    return jnp.cos(ang), jnp.sin(ang)


def kernel(x, ln_mix_g, w_in, ret_log_gamma, ssm_a_re, ssm_a_im, ssm_log_dt, ssm_b_re, ssm_b_im,
           ssm_c_re, ssm_c_im, ssm_d, w_glu, b_glu, w_out, ln_ffn_g, w_ffn_gate, w_ffn_up,
           w_ffn_down, ln_final_g):
    depth = w_in.shape[0]
    cos, sin = _rotary_tables()
    row = lambda p: p.astype(F32).reshape(depth, 1, D_MODEL)
    mix_gain, ffn_gain, glu_bias = row(ln_mix_g), row(ln_ffn_g), row(b_glu)
    d_t = jnp.tile(ssm_d.astype(F32).reshape(depth, SSM_GROUPS, 1, SSM_GROUP), (1, 1, 1, SSM_T))
    final_gain = ln_final_g.astype(F32).reshape(1, D_MODEL)
    log_gamma = ret_log_gamma.astype(F32)
    w_in_b, w_out_b = _column_blocks(w_in, TN), _column_blocks(w_out, TN)
    w_gate_b, w_up_b = _column_blocks(w_ffn_gate, FFN_TF), _column_blocks(w_ffn_up, FFN_TF)
    w_glu_b, w_down_b = w_glu.astype(BF16), w_ffn_down.astype(BF16)
    fac, coeffs = _ssm_factors(ssm_a_re, ssm_a_im, ssm_log_dt, ssm_b_re, ssm_b_im,
                               ssm_c_re, ssm_c_im)
    w_intra, w_state, w_inter_t = _ssm_weights(fac)

    xt = x.reshape(TOKENS, D_MODEL)
    for i in range(depth):
        proj3, u_t = _norm_proj(xt.reshape(BATCH, SEQ, D_MODEL), mix_gain, w_in_b, i)
        ret_g3 = _retention(proj3, log_gamma, cos, sin, i)
        a_lb = _ssm_unpack(_ssm(u_t, w_intra, w_state, w_inter_t, coeffs, d_t, i))
        merged3 = _glu_merge(a_lb, proj3, w_glu_b, glu_bias, ret_g3, i)
        xt = _proj_residual(merged3.reshape(TOKENS, D_MODEL), w_out_b, xt, i)
        xt = _ffn(xt, ffn_gain, w_gate_b, w_up_b, w_down_b, final_gain, i,
                  final_norm=(i == depth - 1))
    return xt.reshape(BATCH, SEQ, D_MODEL)
```

```python
import functools
import math

import numpy as np
import jax
import jax.numpy as jnp
from jax import lax
from jax.experimental import pallas as pl
from jax.experimental.pallas import tpu as pltpu

F32 = jnp.float32
BF16 = jnp.bfloat16

D_MODEL = 2048
BATCH = 8
SEQ = 2048
TOKENS = BATCH * SEQ
RET_HEADS = 4
RET_QK_DIM = 256
RET_V_DIM = D_MODEL // RET_HEADS
RET_QK_WIDTH = RET_HEADS * RET_QK_DIM
ROPE_BASE = 10000.0
SSM_GROUP = 16
SSM_GROUPS = D_MODEL // SSM_GROUP
SSM_STATE = 64
D_FF = ((8 * D_MODEL // 3 + 255) // 256) * 256
IN_WIDTH = 2 * RET_QK_WIDTH + 5 * D_MODEL
EPS = 1e-6

COL_Q = 0
COL_K = RET_QK_WIDTH
COL_V = 2 * RET_QK_WIDTH
COL_G = COL_V + D_MODEL
COL_U = COL_G + D_MODEL
COL_GATE_R = COL_U + D_MODEL
COL_GATE_S = COL_GATE_R + D_MODEL

LANES = 128
RET_CHUNK = 256
RET_UNROLL = 2
SSM_T = 16
SSM_CHUNKS = SEQ // SSM_T
SSM_ROWS = SSM_CHUNKS * BATCH
SSM_LANES = SSM_T * SSM_GROUP
SSM_GB = 4
SSM_OUT_ROWS = 256
SSM_WGB = 8
GROUPS_PER_VREG = LANES // SSM_GROUP
BLOCKS_PER_VREG = LANES // SSM_GROUP

VMEM_LIMIT = 56 * 1024 * 1024

TM = 1024
PROJ_TL = 128
TN = 1024
GLU_TL = 64
GLU_TN = 512
PACK_CHUNKS = 4
FFN_TM = 1024
FFN_TF = 512


def _params(sem):
    return pltpu.CompilerParams(dimension_semantics=sem, vmem_limit_bytes=VMEM_LIMIT)


def _rms(x, g):
    return x * lax.rsqrt(jnp.mean(x * x, axis=-1, keepdims=True) + EPS) * g


def _sigmoid(x):
    return 0.5 * jnp.tanh(0.5 * x) + 0.5


def _silu(x):
    h = 0.5 * x
    return h + h * jnp.tanh(h)


def _gelu_tanh(x):
    return 0.5 * x * (1.0 + jnp.tanh(math.sqrt(2.0 / math.pi) * (x + 0.044715 * (x * x * x))))


def _norm_proj_kernel(x_ref, g_ref, w_ref, o_ref, ut_ref, h_ref, p_ref):
    j = pl.program_id(1)
    rows = BATCH * PROJ_TL

    @pl.when(j == 0)
    def _():
        h_ref[...] = _rms(x_ref[...].reshape(rows, D_MODEL), g_ref[...]).astype(BF16)

    u_lo, u_hi = COL_U // TN, COL_GATE_R // TN - 1

    def project():
        acc = jnp.dot(h_ref[...], w_ref[...], preferred_element_type=F32)
        o_ref[...] = acc.astype(o_ref.dtype).reshape(BATCH, PROJ_TL, TN)
        return acc

    def stash(acc, slot):
        for k in range(TN // LANES):
            for b in range(BATCH):
                p_ref[slot, k, pl.ds(b, PROJ_TL, stride=BATCH), :] = (
                    acc[b * PROJ_TL:(b + 1) * PROJ_TL, k * LANES:(k + 1) * LANES])

    def relayout(slot):
        blk = _lane_block_ids()
        for k in range(TN // LANES):
            for pair in range(PROJ_TL // SSM_T // 2):
                halves = []
                for c in (2 * pair, 2 * pair + 1):
                    parts = []
                    for s_hi in range(SSM_T // BLOCKS_PER_VREG):
                        v = [p_ref[slot, k, _token_rows(c, s_hi * BLOCKS_PER_VREG + s_lo), :]
                             for s_lo in range(BLOCKS_PER_VREG)]
                        parts.append(_block_transpose(v, blk))
                    halves.append([jnp.concatenate([p[gl] for p in parts], axis=1)
                                   for gl in range(GROUPS_PER_VREG)])
                for gl in range(GROUPS_PER_VREG):
                    ut_ref[k * GROUPS_PER_VREG + gl, pair * 2 * BATCH:(pair + 1) * 2 * BATCH, :] = (
                        jnp.concatenate([halves[0][gl], halves[1][gl]], axis=0).astype(BF16))

    relayout_step = (j > u_lo) & (j <= u_hi + 1)

    @pl.when(jnp.logical_not(relayout_step))
    def _():
        acc = project()

        @pl.when(j == u_lo)
        def _():
            stash(acc, 0)

    @pl.when(relayout_step)
    def _():
        acc = project()
        relayout((j - u_lo - 1) % 2)

        @pl.when(j <= u_hi)
        def _():
            stash(acc, (j - u_lo) % 2)


def _norm_proj(x3, gain, w, layer):
    n = w.shape[-1]
    u_lo, u_hi = COL_U // TN + 1, COL_GATE_R // TN
    rows = BATCH * PROJ_TL
    return pl.pallas_call(
        _norm_proj_kernel,
        grid=(SEQ // PROJ_TL, n // TN),
        in_specs=[pl.BlockSpec((BATCH, PROJ_TL, D_MODEL), lambda i, j: (0, i, 0)),
                  pl.BlockSpec((None, 1, D_MODEL), lambda i, j: (layer, 0, 0)),
                  pl.BlockSpec((None, D_MODEL, TN), lambda i, j: (layer, 0, j))],
        out_specs=[pl.BlockSpec((BATCH, PROJ_TL, TN), lambda i, j: (0, i, j)),
                   pl.BlockSpec((TN // SSM_GROUP, rows // SSM_T, SSM_LANES),
                                lambda i, j: (jnp.clip(j, u_lo, u_hi) - u_lo, i, 0))],
        out_shape=[jax.ShapeDtypeStruct((BATCH, SEQ, n), BF16),
                   jax.ShapeDtypeStruct((SSM_GROUPS, SSM_ROWS, SSM_LANES), BF16)],
        scratch_shapes=[pltpu.VMEM((rows, D_MODEL), BF16),
                        pltpu.VMEM((2, TN // LANES, rows, LANES), F32)],
        compiler_params=_params(("parallel", "arbitrary")),
        name="norm_in_proj",
    )(x3, gain, w)


def _retention_kernel(lg_ref, q_ref, k_ref, v_ref, g_ref, gr_ref, cos_ref, sin_ref, o_ref,
                      qs_ref, ks_ref, ys_ref, sf_ref, sb_ref, *, layer):
    c_len = RET_CHUNK
    n_chunks = SEQ // c_len
    half = RET_QK_DIM // 2
    head = pl.program_id(1)
    lg_f = lg_ref[layer, 0, head]
    lg_b = lg_ref[layer, 1, head]

    cos = cos_ref[...]
    sin = sin_ref[...]
    q = q_ref[...].astype(F32)
    k = k_ref[...].astype(F32) * (RET_QK_DIM ** -0.5)
    qs_ref[:, :half] = q[:, :half] * cos - q[:, half:] * sin
    qs_ref[:, half:] = q[:, :half] * sin + q[:, half:] * cos
    ks_ref[:, :half] = k[:, :half] * cos - k[:, half:] * sin
    ks_ref[:, half:] = k[:, :half] * sin + k[:, half:] * cos

    t_row = lax.broadcasted_iota(jnp.int32, (c_len, RET_QK_DIM), 0).astype(F32)
    q_scale_f = jnp.exp(lg_f * (t_row + 1.0))
    q_scale_b = jnp.exp(lg_b * (c_len - t_row))
    k_scale_f = jnp.exp(lg_f * (c_len - 1.0 - t_row))
    k_scale_b = jnp.exp(lg_b * t_row)
    diff = (lax.broadcasted_iota(jnp.int32, (c_len, c_len), 0)
            - lax.broadcasted_iota(jnp.int32, (c_len, c_len), 1)).astype(F32)
    dmat = jnp.exp(jnp.where(diff >= 0, lg_f * diff, -lg_b * diff))
    decay_f = jnp.exp(jnp.full((1, RET_V_DIM), lg_f * c_len, F32))
    decay_b = jnp.exp(jnp.full((1, RET_V_DIM), lg_b * c_len, F32))

    sf_ref[...] = jnp.zeros_like(sf_ref)
    sb_ref[...] = jnp.zeros_like(sb_ref)

    def rows(c):
        return pl.ds(pl.multiple_of(c * c_len, c_len), c_len)

    def forward(c, carry):
        r = rows(c)
        qc = qs_ref[r, :]
        kc = ks_ref[r, :]
        vc = v_ref[r, :]
        scores = lax.dot_general(qc.astype(BF16), kc.astype(BF16), (((1,), (1,)), ((), ())),
                                 preferred_element_type=F32) * dmat
        y = jnp.dot(scores.astype(BF16), vc, preferred_element_type=F32)
        y = y + jnp.dot((qc * q_scale_f).astype(BF16), sf_ref[...].astype(BF16),
                        preferred_element_type=F32)
        ys_ref[r, :] = y
        kv = lax.dot_general((kc * k_scale_f).astype(BF16), vc, (((0,), (0,)), ((), ())),
                             preferred_element_type=F32)
        sf_ref[...] = decay_f * sf_ref[...] + kv
        return carry

    lax.fori_loop(0, n_chunks, forward, 0, unroll=RET_UNROLL)

    def backward(i, carry):
        c = n_chunks - 1 - i
        r = rows(c)
        qc = qs_ref[r, :]
        kc = ks_ref[r, :]
        vc = v_ref[r, :]
        y = ys_ref[r, :] + jnp.dot((qc * q_scale_b).astype(BF16), sb_ref[...].astype(BF16),
                                   preferred_element_type=F32)
        kv = lax.dot_general((kc * k_scale_b).astype(BF16), vc, (((0,), (0,)), ((), ())),
                             preferred_element_type=F32)
        sb_ref[...] = decay_b * sb_ref[...] + kv
        yn = y * lax.rsqrt(jnp.mean(y * y, axis=-1, keepdims=True) + EPS)
        g = g_ref[r, :].astype(F32)
        gr = gr_ref[r, :].astype(F32)
        o_ref[r, :] = (_sigmoid(gr) * (_silu(g) * yn)).astype(o_ref.dtype)
        return carry

    lax.fori_loop(0, n_chunks, backward, 0, unroll=RET_UNROLL)


def _retention(proj3, log_gamma, cos, sin, layer):
    qk_blk = RET_QK_DIM
    v_blk = RET_V_DIM
    return pl.pallas_call(
        functools.partial(_retention_kernel, layer=layer),
        grid=(BATCH, RET_HEADS),
        in_specs=[
            pl.BlockSpec(memory_space=pltpu.SMEM),
            pl.BlockSpec((None, SEQ, qk_blk), lambda b, h: (b, 0, COL_Q // qk_blk + h)),
            pl.BlockSpec((None, SEQ, qk_blk), lambda b, h: (b, 0, COL_K // qk_blk + h)),
            pl.BlockSpec((None, SEQ, v_blk), lambda b, h: (b, 0, COL_V // v_blk + h)),
            pl.BlockSpec((None, SEQ, v_blk), lambda b, h: (b, 0, COL_G // v_blk + h)),
            pl.BlockSpec((None, SEQ, v_blk), lambda b, h: (b, 0, COL_GATE_R // v_blk + h)),
            pl.BlockSpec((SEQ, RET_QK_DIM // 2), lambda b, h: (0, 0)),
            pl.BlockSpec((SEQ, RET_QK_DIM // 2), lambda b, h: (0, 0)),
        ],
        out_specs=pl.BlockSpec((None, SEQ, v_blk), lambda b, h: (b, 0, h)),
        out_shape=jax.ShapeDtypeStruct((BATCH, SEQ, D_MODEL), BF16),
        scratch_shapes=[pltpu.VMEM((SEQ, RET_QK_DIM), F32),
                        pltpu.VMEM((SEQ, RET_QK_DIM), F32),
                        pltpu.VMEM((SEQ, RET_V_DIM), F32),
                        pltpu.VMEM((RET_QK_DIM, RET_V_DIM), F32),
                        pltpu.VMEM((RET_QK_DIM, RET_V_DIM), F32)],
        compiler_params=_params(("parallel", "arbitrary")),
        name="retention",
    )(log_gamma, proj3, proj3, proj3, proj3, proj3, cos, sin)


def _ssm_factors(a_re, a_im, log_dt, b_re, b_im, c_re, c_im):
    t_len, p = SSM_T, SSM_STATE
    depth = a_re.shape[0]
    ar_t = np.arange(t_len, dtype=np.float32)
    n_state = np.stack([t_len - 1.0 - ar_t, ar_t])
    n_out = np.stack([ar_t + 1.0, t_len - ar_t])
    dt = jnp.exp(log_dt)[..., None]
    lr, li = a_re, a_im
    lrdt, lidt = lr * dt, li * dt

    def powers(n):
        e = n[None, :, None, :, None]
        mag = jnp.exp(e * lrdt[:, :, :, None, :])
        ang = e * lidt[:, :, :, None, :]
        return mag * jnp.cos(ang), mag * jnp.sin(ang)

    mag1 = jnp.exp(lrdt)
    lbr, lbi = mag1 * jnp.cos(lidt), mag1 * jnp.sin(lidt)
    den = lr * lr + li * li
    nr, ni = lbr - 1.0, lbi
    fr, fi = (nr * lr + ni * li) / den, (ni * lr - nr * li) / den
    bt_r = jnp.swapaxes(b_re, -1, -2)
    bt_i = jnp.swapaxes(b_im, -1, -2)
    bbr = fr[..., None, :] * bt_r - fi[..., None, :] * bt_i
    bbi = fr[..., None, :] * bt_i + fi[..., None, :] * bt_r

    pr, pi = powers(n_state)
    qr, qi = powers(n_out)
    cat = lambda a, b: jnp.concatenate([a, b], axis=-1)
    fac = jnp.stack([cat(pr, pr), cat(pi, pi), cat(bbr, bbi), cat(-bbi, bbr),
                     cat(qr, qr), cat(qi, qi), cat(c_re, -c_im), cat(-c_im, -c_re)],
                    axis=3).astype(F32)

    mag_t = jnp.exp(t_len * lrdt)
    ar, ai = mag_t * jnp.cos(t_len * lidt), mag_t * jnp.sin(t_len * lidt)
    a1 = jnp.concatenate([ar, ar], axis=-1)
    a2 = jnp.concatenate([-ai, ai], axis=-1)
    coeffs = jnp.stack([a1, a2, -a2], axis=3)
    coeffs = coeffs.transpose(0, 2, 1, 3, 4).reshape(depth, SSM_GROUPS, 6, 2 * p)
    coeffs = jnp.broadcast_to(coeffs[:, :, :, None, :], (depth, SSM_GROUPS, 6, BATCH, 2 * p))
    return fac, coeffs.astype(F32)


def _expand_factors(f_ref, g):
    p1, p2, b1, b2, q1, q2, c1, c2 = [f_ref[g, n] for n in range(8)]
    s_cat = jnp.concatenate([p1[s:s + 1, :] * b1 + p2[s:s + 1, :] * b2 for s in range(SSM_T)], axis=0)
    i_cat = jnp.concatenate([q1[t:t + 1, :] * c1 + q2[t:t + 1, :] * c2 for t in range(SSM_T)], axis=0)
    c_tile = jnp.concatenate([c1] * SSM_T, axis=0)
    return s_cat, i_cat, c_tile


def _ssm_weights_kernel(ff_ref, fb_ref, wi_ref, ws_ref, wx_ref):
    nt = (((1,), (1,)), ((), ()))
    blk = lax.broadcasted_iota(jnp.int32, (SSM_LANES, LANES), 1) // SSM_GROUP
    for g in range(SSM_WGB):
        sf, i_f, cf = _expand_factors(ff_ref, g)
        sb, i_b, cb = _expand_factors(fb_ref, g)
        kf = lax.dot_general(sf, cf, nt, precision=lax.Precision.HIGHEST,
                             preferred_element_type=F32)
        kb = lax.dot_general(sb, cb, nt, precision=lax.Precision.HIGHEST,
                             preferred_element_type=F32)
        halves = []
        for half in range(SSM_LANES // LANES):
            lanes = slice(half * LANES, (half + 1) * LANES)
            acc = jnp.zeros((SSM_LANES, LANES), F32)
            for t_lo in range(BLOCKS_PER_VREG):
                t = half * BLOCKS_PER_VREG + t_lo
                up = (SSM_T - 1 - t) * SSM_GROUP
                down = t * SSM_GROUP
                parts_f = [kf[up:, lanes]] + ([jnp.zeros((up, LANES), F32)] if up else [])
                parts_b = ([jnp.zeros((down, LANES), F32)] if down else []) + [kb[:SSM_LANES - down, lanes]]
                shifted = jnp.concatenate(parts_f, axis=0) + jnp.concatenate(parts_b, axis=0)
                acc = jnp.where(blk == t_lo, shifted, acc)
            halves.append(acc)
        wi_ref[g] = jnp.concatenate(halves, axis=1).astype(wi_ref.dtype)
        ws_ref[g] = jnp.concatenate([sf, pltpu.roll(sf, SSM_STATE, axis=1),
                                     sb, pltpu.roll(sb, SSM_STATE, axis=1)], axis=1).astype(ws_ref.dtype)
        wx_ref[g] = jnp.concatenate([i_f, i_b], axis=1).astype(wx_ref.dtype)


def _ssm_weights(fac):
    depth = fac.shape[0]
    gb = SSM_WGB
    blk = (None, None, gb) + fac.shape[3:]
    fwd = pl.BlockSpec(blk, lambda l, i: (l, 0, i, 0, 0, 0))
    bwd = pl.BlockSpec(blk, lambda l, i: (l, 1, i, 0, 0, 0))

    def out(n):
        return (pl.BlockSpec((None, gb, SSM_LANES, n), lambda l, i: (l, i, 0, 0)),
                jax.ShapeDtypeStruct((depth, SSM_GROUPS, SSM_LANES, n), BF16))

    specs, shapes = zip(out(SSM_LANES), out(8 * SSM_STATE), out(4 * SSM_STATE))
    return pl.pallas_call(
        _ssm_weights_kernel,
        grid=(depth, SSM_GROUPS // gb),
        in_specs=[fwd, bwd],
        out_specs=list(specs),
        out_shape=list(shapes),
        compiler_params=_params(("parallel", "parallel")),
        name="ssm_weights",
    )(fac, fac)


def _lane_block_ids():
    return lax.broadcasted_iota(jnp.int32, (BATCH, LANES), 1) // SSM_GROUP


def _token_rows(c, t):
    first = (c * SSM_T + t) * BATCH
    return pl.ds(first if isinstance(first, int) else pl.multiple_of(first, BATCH), BATCH)


def _block_transpose(src, blk):
    n = BLOCKS_PER_VREG
    rolled = []
    for d in range(n):
        m = src[d % n]
        for a in range(1, n):
            m = jnp.where(blk == a, src[(a + d) % n], m)
        rolled.append(pltpu.roll(m, d * SSM_GROUP, axis=1) if d else m)
    out = []
    for a in range(n):
        t = rolled[(0 - a) % n]
        for b in range(1, n):
            t = jnp.where(blk == b, rolled[(b - a) % n], t)
        out.append(t)
    return out


def _ssm_unpack_kernel(y_ref, o_ref):
    blk = _lane_block_ids()

    def body(it, carry):
        for pair in range(PACK_CHUNKS // 2):
            first = (it * (PACK_CHUNKS // 2) + pair) * 2 * BATCH
            rows = pl.ds(pl.multiple_of(first, 2 * BATCH), 2 * BATCH)
            y = [y_ref[gl, rows, :].astype(F32) for gl in range(GROUPS_PER_VREG)]
            for k in range(2):
                c = it * PACK_CHUNKS + 2 * pair + k
                for t_hi in range(SSM_T // BLOCKS_PER_VREG):
                    src = [y[gl][k * BATCH:(k + 1) * BATCH, t_hi * LANES:(t_hi + 1) * LANES]
                           for gl in range(GROUPS_PER_VREG)]
                    out = _block_transpose(src, blk)
                    for t_lo in range(BLOCKS_PER_VREG):
                        o_ref[_token_rows(c, t_hi * BLOCKS_PER_VREG + t_lo), :] = out[t_lo]
        return carry

    lax.fori_loop(0, SSM_CHUNKS // PACK_CHUNKS, body, 0)


def _ssm_unpack(a_t):
    return pl.pallas_call(
        _ssm_unpack_kernel,
        grid=(D_MODEL // LANES,),
        in_specs=[pl.BlockSpec((GROUPS_PER_VREG, SSM_ROWS, SSM_LANES), lambda i: (i, 0, 0))],
        out_specs=pl.BlockSpec((None, TOKENS, LANES), lambda i: (i, 0, 0)),
        out_shape=jax.ShapeDtypeStruct((D_MODEL // LANES, TOKENS, LANES), F32),
        compiler_params=_params(("parallel",)),
        name="ssm_unpack",
    )(a_t)


def _ssm_kernel(u_ref, wi_ref, ws_ref, wx_ref, a_ref, d_ref, o_ref, s_ref, x_ref):
    p2 = 2 * SSM_STATE
    nt = (((1,), (1,)), ((), ()))
    for g in range(SSM_GB):
        s_ref[g] = jnp.dot(u_ref[g], ws_ref[g], preferred_element_type=F32)

    def step(c, carry):
        rf = pl.ds(pl.multiple_of(c * BATCH, BATCH), BATCH)
        rb = pl.ds(pl.multiple_of((SSM_CHUNKS - 1 - c) * BATCH, BATCH), BATCH)
        new = []
        for g in range(SSM_GB):
            xf, xfs, xb, xbs = carry[4 * g:4 * g + 4]
            x_ref[g, rf, 0:p2] = xf
            x_ref[g, rb, p2:2 * p2] = xb
            nxf = a_ref[g, 0] * xf + a_ref[g, 1] * xfs + s_ref[g, rf, 0:p2]
            nxfs = a_ref[g, 0] * xfs + a_ref[g, 2] * xf + s_ref[g, rf, p2:2 * p2]
            nxb = a_ref[g, 3] * xb + a_ref[g, 4] * xbs + s_ref[g, rb, 2 * p2:3 * p2]
            nxbs = a_ref[g, 3] * xbs + a_ref[g, 5] * xb + s_ref[g, rb, 3 * p2:4 * p2]
            new += [nxf, nxfs, nxb, nxbs]
        return tuple(new)

    zero = jnp.zeros((BATCH, p2), F32)
    lax.fori_loop(0, SSM_CHUNKS, step, (zero,) * (4 * SSM_GB))

    for g in range(SSM_GB):
        for r in range(SSM_ROWS // SSM_OUT_ROWS):
            rows = slice(r * SSM_OUT_ROWS, (r + 1) * SSM_OUT_ROWS)
            u = u_ref[g, rows, :]
            y = jnp.dot(u, wi_ref[g], preferred_element_type=F32)
            y = y + lax.dot_general(x_ref[g, rows, :].astype(BF16), wx_ref[g], nt,
                                    preferred_element_type=F32)
            o_ref[g, rows, :] = _gelu_tanh(y + d_ref[g] * u.astype(F32)).astype(o_ref.dtype)


def _ssm(u_t, w_intra, w_state, w_inter_t, coeffs, d_t, layer):
    gb = SSM_GB

    def wspec(n):
        return pl.BlockSpec((None, gb, SSM_LANES, n), lambda i: (layer, i, 0, 0))

    return pl.pallas_call(
        _ssm_kernel,
        grid=(SSM_GROUPS // gb,),
        in_specs=[pl.BlockSpec((gb, SSM_ROWS, SSM_LANES), lambda i: (i, 0, 0)),
                  wspec(SSM_LANES), wspec(8 * SSM_STATE), wspec(4 * SSM_STATE),
                  pl.BlockSpec((None, gb, 6, BATCH, 2 * SSM_STATE), lambda i: (layer, i, 0, 0, 0)),
                  pl.BlockSpec((None, gb, 1, SSM_LANES), lambda i: (layer, i, 0, 0))],
        out_specs=pl.BlockSpec((gb, SSM_ROWS, SSM_LANES), lambda i: (i, 0, 0)),
        out_shape=jax.ShapeDtypeStruct((SSM_GROUPS, SSM_ROWS, SSM_LANES), BF16),
        scratch_shapes=[pltpu.VMEM((gb, SSM_ROWS, 8 * SSM_STATE), F32),
                        pltpu.VMEM((gb, SSM_ROWS, 4 * SSM_STATE), F32)],
        compiler_params=_params(("parallel",)),
        name="ssm_core",
    )(u_t, w_intra, w_state, w_inter_t, coeffs, d_t)


def _glu_merge_kernel(a_ref, w_ref, b_ref, gs_ref, ret_ref, o_ref, af_ref, ab_ref):
    rows = BATCH * GLU_TL
    for k in range(D_MODEL // LANES):
        for b in range(BATCH):
            af_ref[b * GLU_TL:(b + 1) * GLU_TL, k * LANES:(k + 1) * LANES] = (
                a_ref[k, pl.ds(b, GLU_TL, stride=BATCH), :])
    ab_ref[...] = af_ref[...].astype(BF16)
    for n in range(D_MODEL // GLU_TN):
        cols = slice(n * GLU_TN, (n + 1) * GLU_TN)
        z = jnp.dot(ab_ref[...], w_ref[:, cols], preferred_element_type=F32) + b_ref[:, cols]
        ssm_out = af_ref[:, cols] * _sigmoid(z)
        gate = _sigmoid(gs_ref[:, :, cols].reshape(rows, GLU_TN).astype(F32))
        merged = ret_ref[:, :, cols].reshape(rows, GLU_TN).astype(F32) + gate * ssm_out
        o_ref[:, :, cols] = merged.astype(o_ref.dtype).reshape(BATCH, GLU_TL, GLU_TN)


def _glu_merge(a_lb, proj3, w_glu, b_glu, ret_g3, layer):
    rows = BATCH * GLU_TL
    tile = pl.BlockSpec((BATCH, GLU_TL, D_MODEL), lambda i: (0, i, 0))
    return pl.pallas_call(
        _glu_merge_kernel,
        grid=(SEQ // GLU_TL,),
        in_specs=[pl.BlockSpec((D_MODEL // LANES, rows, LANES), lambda i: (0, i, 0)),
                  pl.BlockSpec((None, D_MODEL, D_MODEL), lambda i: (layer, 0, 0),
                               pipeline_mode=pl.Buffered(1)),
                  pl.BlockSpec((None, 1, D_MODEL), lambda i: (layer, 0, 0)),
                  pl.BlockSpec((BATCH, GLU_TL, D_MODEL), lambda i: (0, i, COL_GATE_S // D_MODEL)),
                  tile],
        out_specs=tile,
        out_shape=jax.ShapeDtypeStruct((BATCH, SEQ, D_MODEL), BF16),
        scratch_shapes=[pltpu.VMEM((rows, D_MODEL), F32), pltpu.VMEM((rows, D_MODEL), BF16)],
        compiler_params=_params(("parallel",)),
        name="glu_merge",
    )(a_lb, w_glu, b_glu, proj3, ret_g3)


def _proj_residual_kernel(a_ref, w_ref, x_ref, o_ref):
    o_ref[...] = x_ref[...] + jnp.dot(a_ref[...], w_ref[...], preferred_element_type=F32)


def _proj_residual(a, w, x, layer):
    m, kdim = a.shape
    n = w.shape[-1]
    return pl.pallas_call(
        _proj_residual_kernel,
        grid=(m // TM, n // TN),
        in_specs=[pl.BlockSpec((TM, kdim), lambda i, j: (i, 0)),
                  pl.BlockSpec((None, kdim, TN), lambda i, j: (layer, 0, j)),
                  pl.BlockSpec((TM, TN), lambda i, j: (i, j))],
        out_specs=pl.BlockSpec((TM, TN), lambda i, j: (i, j)),
        out_shape=jax.ShapeDtypeStruct((m, n), F32),
        compiler_params=_params(("parallel", "parallel")),
        name="out_proj_residual",
    )(a, w, x)


def _ffn_kernel(x_ref, g_ref, wg_ref, wu_ref, wd_ref, gf_ref, o_ref, h_ref, *, final_norm):
    j = pl.program_id(1)

    @pl.when(j == 0)
    def _():
        x = x_ref[...]
        h_ref[...] = _rms(x, g_ref[...]).astype(BF16)
        o_ref[...] = x

    h = h_ref[...]
    gate = jnp.dot(h, wg_ref[...], preferred_element_type=F32)
    up = jnp.dot(h, wu_ref[...], preferred_element_type=F32)
    act = (_silu(gate) * up).astype(BF16)
    o_ref[...] += jnp.dot(act, wd_ref[...], preferred_element_type=F32)

    if final_norm:
        @pl.when(j == pl.num_programs(1) - 1)
        def _():
            o_ref[...] = _rms(o_ref[...], gf_ref[...])


def _ffn(x, gain, w_gate, w_up, w_down, final_gain, layer, final_norm):
    m, d = x.shape
    f = w_gate.shape[-1]
    return pl.pallas_call(
        functools.partial(_ffn_kernel, final_norm=final_norm),
        grid=(m // FFN_TM, f // FFN_TF),
        in_specs=[pl.BlockSpec((FFN_TM, d), lambda i, j: (i, 0)),
                  pl.BlockSpec((None, 1, d), lambda i, j: (layer, 0, 0)),
                  pl.BlockSpec((None, d, FFN_TF), lambda i, j: (layer, 0, j)),
                  pl.BlockSpec((None, d, FFN_TF), lambda i, j: (layer, 0, j)),
                  pl.BlockSpec((None, FFN_TF, d), lambda i, j: (layer, j, 0)),
                  pl.BlockSpec((1, d), lambda i, j: (0, 0))],
        out_specs=pl.BlockSpec((FFN_TM, d), lambda i, j: (i, 0)),
        out_shape=jax.ShapeDtypeStruct((m, d), F32),
        scratch_shapes=[pltpu.VMEM((FFN_TM, d), BF16)],
        compiler_params=_params(("parallel", "arbitrary")),
        name="ffn",
    )(x, gain, w_gate, w_up, w_down, final_gain)


def _rotary_tables():
    half = RET_QK_DIM // 2
    inv = 1.0 / (ROPE_BASE ** (jnp.arange(half, dtype=F32) / half))
    ang = jnp.arange(SEQ, dtype=F32)[:, None] * inv[None, :]
    return jnp.cos(ang), jnp.sin(ang)


def kernel(x, ln_mix_g, w_in, ret_log_gamma, ssm_a_re, ssm_a_im, ssm_log_dt, ssm_b_re, ssm_b_im,
           ssm_c_re, ssm_c_im, ssm_d, w_glu, b_glu, w_out, ln_ffn_g, w_ffn_gate, w_ffn_up,
           w_ffn_down, ln_final_g):
    depth = w_in.shape[0]
    cos, sin = _rotary_tables()
    row = lambda p: p.astype(F32).reshape(depth, 1, D_MODEL)
    mix_gain, ffn_gain, glu_bias = row(ln_mix_g), row(ln_ffn_g), row(b_glu)
    d_t = jnp.tile(ssm_d.astype(F32).reshape(depth, SSM_GROUPS, 1, SSM_GROUP), (1, 1, 1, SSM_T))
    final_gain = ln_final_g.astype(F32).reshape(1, D_MODEL)
    log_gamma = ret_log_gamma.astype(F32)
    w_in_b, w_glu_b, w_out_b = w_in.astype(BF16), w_glu.astype(BF16), w_out.astype(BF16)
    w_gate_b, w_up_b, w_down_b = (w_ffn_gate.astype(BF16), w_ffn_up.astype(BF16),
                                  w_ffn_down.astype(BF16))
    fac, coeffs = _ssm_factors(ssm_a_re, ssm_a_im, ssm_log_dt, ssm_b_re, ssm_b_im,
                               ssm_c_re, ssm_c_im)
    w_intra, w_state, w_inter_t = _ssm_weights(fac)

    xt = x.reshape(TOKENS, D_MODEL)
    for i in range(depth):
        proj3, u_t = _norm_proj(xt.reshape(BATCH, SEQ, D_MODEL), mix_gain, w_in_b, i)
        ret_g3 = _retention(proj3, log_gamma, cos, sin, i)
        a_lb = _ssm_unpack(_ssm(u_t, w_intra, w_state, w_inter_t, coeffs, d_t, i))
        merged3 = _glu_merge(a_lb, proj3, w_glu_b, glu_bias, ret_g3, i)
        xt = _proj_residual(merged3.reshape(TOKENS, D_MODEL), w_out_b, xt, i)
        xt = _ffn(xt, ffn_gain, w_gate_b, w_up_b, w_down_b, final_gain, i,
                  final_norm=(i == depth - 1))
    return xt.reshape(BATCH, SEQ, D_MODEL)
```

```python
import functools
import math

import numpy as np
import jax
import jax.numpy as jnp
from jax import lax
from jax.experimental import pallas as pl
from jax.experimental.pallas import tpu as pltpu

F32 = jnp.float32
BF16 = jnp.bfloat16

D_MODEL = 2048
BATCH = 8
SEQ = 2048
TOKENS = BATCH * SEQ
RET_HEADS = 4
RET_QK_DIM = 256
RET_V_DIM = D_MODEL // RET_HEADS
RET_QK_WIDTH = RET_HEADS * RET_QK_DIM
ROPE_BASE = 10000.0
SSM_GROUP = 16
SSM_GROUPS = D_MODEL // SSM_GROUP
SSM_STATE = 64
D_FF = ((8 * D_MODEL // 3 + 255) // 256) * 256
IN_WIDTH = 2 * RET_QK_WIDTH + 5 * D_MODEL
EPS = 1e-6

COL_Q = 0
COL_K = RET_QK_WIDTH
COL_V = 2 * RET_QK_WIDTH
COL_G = COL_V + D_MODEL
COL_U = COL_G + D_MODEL
COL_GATE_R = COL_U + D_MODEL
COL_GATE_S = COL_GATE_R + D_MODEL

LANES = 128
RET_CHUNK = 256
RET_UNROLL = 2
SSM_T = 16
SSM_CHUNKS = SEQ // SSM_T
SSM_ROWS = SSM_CHUNKS * BATCH
SSM_LANES = SSM_T * SSM_GROUP
SSM_GB = 4
SSM_OUT_ROWS = 256
SSM_WGB = 8
GROUPS_PER_VREG = LANES // SSM_GROUP
BLOCKS_PER_VREG = LANES // SSM_GROUP
TRANSPOSE_DIRECT = 4

VMEM_LIMIT = 56 * 1024 * 1024

TM = 1024
PROJ_TL = 128
TN = 1024
GLU_TL = 64
GLU_TN = 512
PACK_CHUNKS = 4
FFN_TM = 1024
FFN_TF = 512


def _params(sem):
    return pltpu.CompilerParams(dimension_semantics=sem, vmem_limit_bytes=VMEM_LIMIT)


def _rms(x, g):
    return x * lax.rsqrt(jnp.mean(x * x, axis=-1, keepdims=True) + EPS) * g


def _sigmoid(x):
    return 0.5 * jnp.tanh(0.5 * x) + 0.5


def _silu(x):
    h = 0.5 * x
    return h + h * jnp.tanh(h)


def _gelu_tanh(x):
    return 0.5 * x * (1.0 + jnp.tanh(math.sqrt(2.0 / math.pi) * (x + 0.044715 * (x * x * x))))


def _norm_proj_kernel(x_ref, g_ref, w_ref, o_ref, ut_ref, h_ref, p_ref):
    j = pl.program_id(1)
    rows = BATCH * PROJ_TL

    @pl.when(j == 0)
    def _():
        h_ref[...] = _rms(x_ref[...].reshape(rows, D_MODEL), g_ref[...]).astype(BF16)

    u_lo, u_hi = COL_U // TN, COL_GATE_R // TN - 1

    def project():
        acc = jnp.dot(h_ref[...], w_ref[...], preferred_element_type=F32)
        o_ref[...] = acc.astype(o_ref.dtype).reshape(BATCH, PROJ_TL, TN)
        return acc

    def stash(acc, slot):
        for k in range(TN // LANES):
            for b in range(BATCH):
                p_ref[slot, k, pl.ds(b, PROJ_TL, stride=BATCH), :] = (
                    acc[b * PROJ_TL:(b + 1) * PROJ_TL, k * LANES:(k + 1) * LANES])

    def relayout(slot):
        blk = _lane_block_ids()
        for k in range(TN // LANES):
            for pair in range(PROJ_TL // SSM_T // 2):
                halves = []
                for c in (2 * pair, 2 * pair + 1):
                    parts = []
                    for s_hi in range(SSM_T // BLOCKS_PER_VREG):
                        v = [p_ref[slot, k, _token_rows(c, s_hi * BLOCKS_PER_VREG + s_lo), :]
                             for s_lo in range(BLOCKS_PER_VREG)]
                        parts.append(_block_transpose(v, blk, TRANSPOSE_DIRECT))
                    halves.append([jnp.concatenate([p[gl] for p in parts], axis=1)
                                   for gl in range(GROUPS_PER_VREG)])
                for gl in range(GROUPS_PER_VREG):
                    ut_ref[k * GROUPS_PER_VREG + gl, pair * 2 * BATCH:(pair + 1) * 2 * BATCH, :] = (
                        jnp.concatenate([halves[0][gl], halves[1][gl]], axis=0).astype(BF16))

    relayout_step = (j > u_lo) & (j <= u_hi + 1)

    @pl.when(jnp.logical_not(relayout_step))
    def _():
        acc = project()

        @pl.when(j == u_lo)
        def _():
            stash(acc, 0)

    @pl.when(relayout_step)
    def _():
        acc = project()
        relayout((j - u_lo - 1) % 2)

        @pl.when(j <= u_hi)
        def _():
            stash(acc, (j - u_lo) % 2)


def _norm_proj(x3, gain, w, layer):
    n = w.shape[-1]
    u_lo, u_hi = COL_U // TN + 1, COL_GATE_R // TN
    rows = BATCH * PROJ_TL
    return pl.pallas_call(
        _norm_proj_kernel,
        grid=(SEQ // PROJ_TL, n // TN),
        in_specs=[pl.BlockSpec((BATCH, PROJ_TL, D_MODEL), lambda i, j: (0, i, 0)),
                  pl.BlockSpec((None, 1, D_MODEL), lambda i, j: (layer, 0, 0)),
                  pl.BlockSpec((None, D_MODEL, TN), lambda i, j: (layer, 0, j))],
        out_specs=[pl.BlockSpec((BATCH, PROJ_TL, TN), lambda i, j: (0, i, j)),
                   pl.BlockSpec((TN // SSM_GROUP, rows // SSM_T, SSM_LANES),
                                lambda i, j: (jnp.clip(j, u_lo, u_hi) - u_lo, i, 0))],
        out_shape=[jax.ShapeDtypeStruct((BATCH, SEQ, n), BF16),
                   jax.ShapeDtypeStruct((SSM_GROUPS, SSM_ROWS, SSM_LANES), BF16)],
        scratch_shapes=[pltpu.VMEM((rows, D_MODEL), BF16),
                        pltpu.VMEM((2, TN // LANES, rows, LANES), F32)],
        compiler_params=_params(("parallel", "arbitrary")),
        name="norm_in_proj",
    )(x3, gain, w)


def _retention_kernel(lg_ref, q_ref, k_ref, v_ref, g_ref, gr_ref, cos_ref, sin_ref, o_ref,
                      qs_ref, ks_ref, ys_ref, sf_ref, sb_ref, *, layer):
    c_len = RET_CHUNK
    n_chunks = SEQ // c_len
    half = RET_QK_DIM // 2
    head = pl.program_id(1)
    lg_f = lg_ref[layer, 0, head]
    lg_b = lg_ref[layer, 1, head]

    cos = cos_ref[...]
    sin = sin_ref[...]
    q = q_ref[...].astype(F32)
    k = k_ref[...].astype(F32) * (RET_QK_DIM ** -0.5)
    qs_ref[:, :half] = q[:, :half] * cos - q[:, half:] * sin
    qs_ref[:, half:] = q[:, :half] * sin + q[:, half:] * cos
    ks_ref[:, :half] = k[:, :half] * cos - k[:, half:] * sin
    ks_ref[:, half:] = k[:, :half] * sin + k[:, half:] * cos

    t_row = lax.broadcasted_iota(jnp.int32, (c_len, RET_QK_DIM), 0).astype(F32)
    q_scale_f = jnp.exp(lg_f * (t_row + 1.0))
    q_scale_b = jnp.exp(lg_b * (c_len - t_row))
    k_scale_f = jnp.exp(lg_f * (c_len - 1.0 - t_row))
    k_scale_b = jnp.exp(lg_b * t_row)
    diff = (lax.broadcasted_iota(jnp.int32, (c_len, c_len), 0)
            - lax.broadcasted_iota(jnp.int32, (c_len, c_len), 1)).astype(F32)
    dmat = jnp.exp(jnp.where(diff >= 0, lg_f * diff, -lg_b * diff))
    decay_f = jnp.exp(jnp.full((1, RET_V_DIM), lg_f * c_len, F32))
    decay_b = jnp.exp(jnp.full((1, RET_V_DIM), lg_b * c_len, F32))

    sf_ref[...] = jnp.zeros_like(sf_ref)
    sb_ref[...] = jnp.zeros_like(sb_ref)

    def rows(c):
        return pl.ds(pl.multiple_of(c * c_len, c_len), c_len)

    def forward(c, carry):
        r = rows(c)
        qc = qs_ref[r, :]
        kc = ks_ref[r, :]
        vc = v_ref[r, :]
        scores = lax.dot_general(qc.astype(BF16), kc.astype(BF16), (((1,), (1,)), ((), ())),
                                 preferred_element_type=F32) * dmat
        y = jnp.dot(scores.astype(BF16), vc, preferred_element_type=F32)
        y = y + jnp.dot((qc * q_scale_f).astype(BF16), sf_ref[...].astype(BF16),
                        preferred_element_type=F32)
        ys_ref[r, :] = y
        kv = lax.dot_general((kc * k_scale_f).astype(BF16), vc, (((0,), (0,)), ((), ())),
                             preferred_element_type=F32)
        sf_ref[...] = decay_f * sf_ref[...] + kv
        return carry

    lax.fori_loop(0, n_chunks, forward, 0, unroll=RET_UNROLL)

    def backward(i, carry):
        c = n_chunks - 1 - i
        r = rows(c)
        qc = qs_ref[r, :]
        kc = ks_ref[r, :]
        vc = v_ref[r, :]
        y = ys_ref[r, :] + jnp.dot((qc * q_scale_b).astype(BF16), sb_ref[...].astype(BF16),
                                   preferred_element_type=F32)
        kv = lax.dot_general((kc * k_scale_b).astype(BF16), vc, (((0,), (0,)), ((), ())),
                             preferred_element_type=F32)
        sb_ref[...] = decay_b * sb_ref[...] + kv
        yn = y * lax.rsqrt(jnp.mean(y * y, axis=-1, keepdims=True) + EPS)
        g = g_ref[r, :].astype(F32)
        gr = gr_ref[r, :].astype(F32)
        o_ref[r, :] = (_sigmoid(gr) * (_silu(g) * yn)).astype(o_ref.dtype)
        return carry

    lax.fori_loop(0, n_chunks, backward, 0, unroll=RET_UNROLL)


def _retention(proj3, log_gamma, cos, sin, layer):
    qk_blk = RET_QK_DIM
    v_blk = RET_V_DIM
    return pl.pallas_call(
        functools.partial(_retention_kernel, layer=layer),
        grid=(BATCH, RET_HEADS),
        in_specs=[
            pl.BlockSpec(memory_space=pltpu.SMEM),
            pl.BlockSpec((None, SEQ, qk_blk), lambda b, h: (b, 0, COL_Q // qk_blk + h)),
            pl.BlockSpec((None, SEQ, qk_blk), lambda b, h: (b, 0, COL_K // qk_blk + h)),
            pl.BlockSpec((None, SEQ, v_blk), lambda b, h: (b, 0, COL_V // v_blk + h)),
            pl.BlockSpec((None, SEQ, v_blk), lambda b, h: (b, 0, COL_G // v_blk + h)),
            pl.BlockSpec((None, SEQ, v_blk), lambda b, h: (b, 0, COL_GATE_R // v_blk + h)),
            pl.BlockSpec((SEQ, RET_QK_DIM // 2), lambda b, h: (0, 0)),
            pl.BlockSpec((SEQ, RET_QK_DIM // 2), lambda b, h: (0, 0)),
        ],
        out_specs=pl.BlockSpec((None, SEQ, v_blk), lambda b, h: (b, 0, h)),
        out_shape=jax.ShapeDtypeStruct((BATCH, SEQ, D_MODEL), BF16),
        scratch_shapes=[pltpu.VMEM((SEQ, RET_QK_DIM), F32),
                        pltpu.VMEM((SEQ, RET_QK_DIM), F32),
                        pltpu.VMEM((SEQ, RET_V_DIM), F32),
                        pltpu.VMEM((RET_QK_DIM, RET_V_DIM), F32),
                        pltpu.VMEM((RET_QK_DIM, RET_V_DIM), F32)],
        compiler_params=_params(("parallel", "arbitrary")),
        name="retention",
    )(log_gamma, proj3, proj3, proj3, proj3, proj3, cos, sin)


def _ssm_factors(a_re, a_im, log_dt, b_re, b_im, c_re, c_im):
    t_len, p = SSM_T, SSM_STATE
    depth = a_re.shape[0]
    ar_t = np.arange(t_len, dtype=np.float32)
    n_state = np.stack([t_len - 1.0 - ar_t, ar_t])
    n_out = np.stack([ar_t + 1.0, t_len - ar_t])
    dt = jnp.exp(log_dt)[..., None]
    lr, li = a_re, a_im
    lrdt, lidt = lr * dt, li * dt

    def powers(n):
        e = n[None, :, None, :, None]
        mag = jnp.exp(e * lrdt[:, :, :, None, :])
        ang = e * lidt[:, :, :, None, :]
        return mag * jnp.cos(ang), mag * jnp.sin(ang)

    mag1 = jnp.exp(lrdt)
    lbr, lbi = mag1 * jnp.cos(lidt), mag1 * jnp.sin(lidt)
    den = lr * lr + li * li
    nr, ni = lbr - 1.0, lbi
    fr, fi = (nr * lr + ni * li) / den, (ni * lr - nr * li) / den
    bt_r = jnp.swapaxes(b_re, -1, -2)
    bt_i = jnp.swapaxes(b_im, -1, -2)
    bbr = fr[..., None, :] * bt_r - fi[..., None, :] * bt_i
    bbi = fr[..., None, :] * bt_i + fi[..., None, :] * bt_r

    pr, pi = powers(n_state)
    qr, qi = powers(n_out)
    cat = lambda a, b: jnp.concatenate([a, b], axis=-1)
    fac = jnp.stack([cat(pr, pr), cat(pi, pi), cat(bbr, bbi), cat(-bbi, bbr),
                     cat(qr, qr), cat(qi, qi), cat(c_re, -c_im), cat(-c_im, -c_re)],
                    axis=3).astype(F32)

    mag_t = jnp.exp(t_len * lrdt)
    ar, ai = mag_t * jnp.cos(t_len * lidt), mag_t * jnp.sin(t_len * lidt)
    a1 = jnp.concatenate([ar, ar], axis=-1)
    a2 = jnp.concatenate([-ai, ai], axis=-1)
    coeffs = jnp.stack([a1, a2, -a2], axis=3)
    coeffs = coeffs.transpose(0, 2, 1, 3, 4).reshape(depth, SSM_GROUPS, 6, 2 * p)
    coeffs = jnp.broadcast_to(coeffs[:, :, :, None, :], (depth, SSM_GROUPS, 6, BATCH, 2 * p))
    return fac, coeffs.astype(F32)


def _expand_factors(f_ref, g):
    p1, p2, b1, b2, q1, q2, c1, c2 = [f_ref[g, n] for n in range(8)]
    s_cat = jnp.concatenate([p1[s:s + 1, :] * b1 + p2[s:s + 1, :] * b2 for s in range(SSM_T)], axis=0)
    i_cat = jnp.concatenate([q1[t:t + 1, :] * c1 + q2[t:t + 1, :] * c2 for t in range(SSM_T)], axis=0)
    c_tile = jnp.concatenate([c1] * SSM_T, axis=0)
    return s_cat, i_cat, c_tile


def _ssm_weights_kernel(ff_ref, fb_ref, wi_ref, ws_ref, wx_ref):
    nt = (((1,), (1,)), ((), ()))
    blk = lax.broadcasted_iota(jnp.int32, (SSM_LANES, LANES), 1) // SSM_GROUP
    for g in range(SSM_WGB):
        sf, i_f, cf = _expand_factors(ff_ref, g)
        sb, i_b, cb = _expand_factors(fb_ref, g)
        kf = lax.dot_general(sf, cf, nt, precision=lax.Precision.HIGHEST,
                             preferred_element_type=F32)
        kb = lax.dot_general(sb, cb, nt, precision=lax.Precision.HIGHEST,
                             preferred_element_type=F32)
        halves = []
        for half in range(SSM_LANES // LANES):
            lanes = slice(half * LANES, (half + 1) * LANES)
            acc = jnp.zeros((SSM_LANES, LANES), F32)
            for t_lo in range(BLOCKS_PER_VREG):
                t = half * BLOCKS_PER_VREG + t_lo
                up = (SSM_T - 1 - t) * SSM_GROUP
                down = t * SSM_GROUP
                parts_f = [kf[up:, lanes]] + ([jnp.zeros((up, LANES), F32)] if up else [])
                parts_b = ([jnp.zeros((down, LANES), F32)] if down else []) + [kb[:SSM_LANES - down, lanes]]
                shifted = jnp.concatenate(parts_f, axis=0) + jnp.concatenate(parts_b, axis=0)
                acc = jnp.where(blk == t_lo, shifted, acc)
            halves.append(acc)
        wi_ref[g] = jnp.concatenate(halves, axis=1).astype(wi_ref.dtype)
        ws_ref[g] = jnp.concatenate([sf, pltpu.roll(sf, SSM_STATE, axis=1),
                                     sb, pltpu.roll(sb, SSM_STATE, axis=1)], axis=1).astype(ws_ref.dtype)
        wx_ref[g] = jnp.concatenate([i_f, i_b], axis=1).astype(wx_ref.dtype)


def _ssm_weights(fac):
    depth = fac.shape[0]
    gb = SSM_WGB
    blk = (None, None, gb) + fac.shape[3:]
    fwd = pl.BlockSpec(blk, lambda l, i: (l, 0, i, 0, 0, 0))
    bwd = pl.BlockSpec(blk, lambda l, i: (l, 1, i, 0, 0, 0))

    def out(n):
        return (pl.BlockSpec((None, gb, SSM_LANES, n), lambda l, i: (l, i, 0, 0)),
                jax.ShapeDtypeStruct((depth, SSM_GROUPS, SSM_LANES, n), BF16))

    specs, shapes = zip(out(SSM_LANES), out(8 * SSM_STATE), out(4 * SSM_STATE))
    return pl.pallas_call(
        _ssm_weights_kernel,
        grid=(depth, SSM_GROUPS // gb),
        in_specs=[fwd, bwd],
        out_specs=list(specs),
        out_shape=list(shapes),
        compiler_params=_params(("parallel", "parallel")),
        name="ssm_weights",
    )(fac, fac)


def _lane_block_ids():
    return lax.broadcasted_iota(jnp.int32, (BATCH, LANES), 1) // SSM_GROUP


def _token_rows(c, t):
    first = (c * SSM_T + t) * BATCH
    return pl.ds(first if isinstance(first, int) else pl.multiple_of(first, BATCH), BATCH)


def _block_transpose(src, blk, n_direct=0):
    n = BLOCKS_PER_VREG
    nd = n_direct
    out = []
    for a in range(nd):
        t = None
        for b in range(n):
            shift = ((b - a) % n) * SSM_GROUP
            r = pltpu.roll(src[b], shift, axis=1) if shift else src[b]
            t = r if t is None else jnp.where(blk == b, r, t)
        out.append(t)
    rolled = []
    for d in range(n):
        m = src[(nd + d) % n]
        for a in range(nd + 1, n):
            m = jnp.where(blk == a, src[(a + d) % n], m)
        rolled.append(pltpu.roll(m, d * SSM_GROUP, axis=1) if d else m)
    for a in range(nd, n):
        t = rolled[(0 - a) % n]
        for b in range(1, n):
            t = jnp.where(blk == b, rolled[(b - a) % n], t)
        out.append(t)
    return out


def _ssm_unpack_kernel(y_ref, o_ref):
    blk = _lane_block_ids()

    def body(it, carry):
        for pair in range(PACK_CHUNKS // 2):
            first = (it * (PACK_CHUNKS // 2) + pair) * 2 * BATCH
            rows = pl.ds(pl.multiple_of(first, 2 * BATCH), 2 * BATCH)
            y = [y_ref[gl, rows, :].astype(F32) for gl in range(GROUPS_PER_VREG)]
            for k in range(2):
                c = it * PACK_CHUNKS + 2 * pair + k
                for t_hi in range(SSM_T // BLOCKS_PER_VREG):
                    src = [y[gl][k * BATCH:(k + 1) * BATCH, t_hi * LANES:(t_hi + 1) * LANES]
                           for gl in range(GROUPS_PER_VREG)]
                    out = _block_transpose(src, blk)
                    for t_lo in range(BLOCKS_PER_VREG):
                        o_ref[_token_rows(c, t_hi * BLOCKS_PER_VREG + t_lo), :] = out[t_lo]
        return carry

    lax.fori_loop(0, SSM_CHUNKS // PACK_CHUNKS, body, 0)


def _ssm_unpack(a_t):
    return pl.pallas_call(
        _ssm_unpack_kernel,
        grid=(D_MODEL // LANES,),
        in_specs=[pl.BlockSpec((GROUPS_PER_VREG, SSM_ROWS, SSM_LANES), lambda i: (i, 0, 0))],
        out_specs=pl.BlockSpec((None, TOKENS, LANES), lambda i: (i, 0, 0)),
        out_shape=jax.ShapeDtypeStruct((D_MODEL // LANES, TOKENS, LANES), F32),
        compiler_params=_params(("parallel",)),
        name="ssm_unpack",
    )(a_t)


def _ssm_kernel(u_ref, wi_ref, ws_ref, wx_ref, a_ref, d_ref, o_ref, s_ref, x_ref):
    p2 = 2 * SSM_STATE
    nt = (((1,), (1,)), ((), ()))
    for g in range(SSM_GB):
        s_ref[g] = jnp.dot(u_ref[g], ws_ref[g], preferred_element_type=F32)

    def step(c, carry):
        rf = pl.ds(pl.multiple_of(c * BATCH, BATCH), BATCH)
        rb = pl.ds(pl.multiple_of((SSM_CHUNKS - 1 - c) * BATCH, BATCH), BATCH)
        new = []
        for g in range(SSM_GB):
            xf, xfs, xb, xbs = carry[4 * g:4 * g + 4]
            x_ref[g, rf, 0:p2] = xf
            x_ref[g, rb, p2:2 * p2] = xb
            nxf = a_ref[g, 0] * xf + a_ref[g, 1] * xfs + s_ref[g, rf, 0:p2]
            nxfs = a_ref[g, 0] * xfs + a_ref[g, 2] * xf + s_ref[g, rf, p2:2 * p2]
            nxb = a_ref[g, 3] * xb + a_ref[g, 4] * xbs + s_ref[g, rb, 2 * p2:3 * p2]
            nxbs = a_ref[g, 3] * xbs + a_ref[g, 5] * xb + s_ref[g, rb, 3 * p2:4 * p2]
            new += [nxf, nxfs, nxb, nxbs]
        return tuple(new)

    zero = jnp.zeros((BATCH, p2), F32)
    lax.fori_loop(0, SSM_CHUNKS, step, (zero,) * (4 * SSM_GB))

    for g in range(SSM_GB):
        for r in range(SSM_ROWS // SSM_OUT_ROWS):
            rows = slice(r * SSM_OUT_ROWS, (r + 1) * SSM_OUT_ROWS)
            u = u_ref[g, rows, :]
            y = jnp.dot(u, wi_ref[g], preferred_element_type=F32)
            y = y + lax.dot_general(x_ref[g, rows, :].astype(BF16), wx_ref[g], nt,
                                    preferred_element_type=F32)
            o_ref[g, rows, :] = _gelu_tanh(y + d_ref[g] * u.astype(F32)).astype(o_ref.dtype)


def _ssm(u_t, w_intra, w_state, w_inter_t, coeffs, d_t, layer):
    gb = SSM_GB

    def wspec(n):
        return pl.BlockSpec((None, gb, SSM_LANES, n), lambda i: (layer, i, 0, 0))

    return pl.pallas_call(
        _ssm_kernel,
        grid=(SSM_GROUPS // gb,),
        in_specs=[pl.BlockSpec((gb, SSM_ROWS, SSM_LANES), lambda i: (i, 0, 0)),
                  wspec(SSM_LANES), wspec(8 * SSM_STATE), wspec(4 * SSM_STATE),
                  pl.BlockSpec((None, gb, 6, BATCH, 2 * SSM_STATE), lambda i: (layer, i, 0, 0, 0)),
                  pl.BlockSpec((None, gb, 1, SSM_LANES), lambda i: (layer, i, 0, 0))],
        out_specs=pl.BlockSpec((gb, SSM_ROWS, SSM_LANES), lambda i: (i, 0, 0)),
        out_shape=jax.ShapeDtypeStruct((SSM_GROUPS, SSM_ROWS, SSM_LANES), BF16),
        scratch_shapes=[pltpu.VMEM((gb, SSM_ROWS, 8 * SSM_STATE), F32),
                        pltpu.VMEM((gb, SSM_ROWS, 4 * SSM_STATE), F32)],
        compiler_params=_params(("parallel",)),
        name="ssm_core",
    )(u_t, w_intra, w_state, w_inter_t, coeffs, d_t)


def _glu_merge_kernel(a_ref, w_ref, b_ref, gs_ref, ret_ref, o_ref, af_ref, ab_ref):
    rows = BATCH * GLU_TL
    for k in range(D_MODEL // LANES):
        for b in range(BATCH):
            af_ref[b * GLU_TL:(b + 1) * GLU_TL, k * LANES:(k + 1) * LANES] = (
                a_ref[k, pl.ds(b, GLU_TL, stride=BATCH), :])
    ab_ref[...] = af_ref[...].astype(BF16)
    for n in range(D_MODEL // GLU_TN):
        cols = slice(n * GLU_TN, (n + 1) * GLU_TN)
        z = jnp.dot(ab_ref[...], w_ref[:, cols], preferred_element_type=F32) + b_ref[:, cols]
        ssm_out = af_ref[:, cols] * _sigmoid(z)
        gate = _sigmoid(gs_ref[:, :, cols].reshape(rows, GLU_TN).astype(F32))
        merged = ret_ref[:, :, cols].reshape(rows, GLU_TN).astype(F32) + gate * ssm_out
        o_ref[:, :, cols] = merged.astype(o_ref.dtype).reshape(BATCH, GLU_TL, GLU_TN)


def _glu_merge(a_lb, proj3, w_glu, b_glu, ret_g3, layer):
    rows = BATCH * GLU_TL
    tile = pl.BlockSpec((BATCH, GLU_TL, D_MODEL), lambda i: (0, i, 0))
    return pl.pallas_call(
        _glu_merge_kernel,
        grid=(SEQ // GLU_TL,),
        in_specs=[pl.BlockSpec((D_MODEL // LANES, rows, LANES), lambda i: (0, i, 0)),
                  pl.BlockSpec((None, D_MODEL, D_MODEL), lambda i: (layer, 0, 0),
                               pipeline_mode=pl.Buffered(1)),
                  pl.BlockSpec((None, 1, D_MODEL), lambda i: (layer, 0, 0)),
                  pl.BlockSpec((BATCH, GLU_TL, D_MODEL), lambda i: (0, i, COL_GATE_S // D_MODEL)),
                  tile],
        out_specs=tile,
        out_shape=jax.ShapeDtypeStruct((BATCH, SEQ, D_MODEL), BF16),
        scratch_shapes=[pltpu.VMEM((rows, D_MODEL), F32), pltpu.VMEM((rows, D_MODEL), BF16)],
        compiler_params=_params(("parallel",)),
        name="glu_merge",
    )(a_lb, w_glu, b_glu, proj3, ret_g3)


def _proj_residual_kernel(a_ref, w_ref, x_ref, o_ref):
    o_ref[...] = x_ref[...] + jnp.dot(a_ref[...], w_ref[...], preferred_element_type=F32)


def _proj_residual(a, w, x, layer):
    m, kdim = a.shape
    n = w.shape[-1]
    return pl.pallas_call(
        _proj_residual_kernel,
        grid=(m // TM, n // TN),
        in_specs=[pl.BlockSpec((TM, kdim), lambda i, j: (i, 0)),
                  pl.BlockSpec((None, kdim, TN), lambda i, j: (layer, 0, j)),
                  pl.BlockSpec((TM, TN), lambda i, j: (i, j))],
        out_specs=pl.BlockSpec((TM, TN), lambda i, j: (i, j)),
        out_shape=jax.ShapeDtypeStruct((m, n), F32),
        compiler_params=_params(("parallel", "parallel")),
        name="out_proj_residual",
    )(a, w, x)


def _ffn_kernel(x_ref, g_ref, wg_ref, wu_ref, wd_ref, gf_ref, o_ref, h_ref, *, final_norm):
    j = pl.program_id(1)

    @pl.when(j == 0)
    def _():
        x = x_ref[...]
        h_ref[...] = _rms(x, g_ref[...]).astype(BF16)
        o_ref[...] = x

    h = h_ref[...]
    gate = jnp.dot(h, wg_ref[...], preferred_element_type=F32)
    up = jnp.dot(h, wu_ref[...], preferred_element_type=F32)
    act = (_silu(gate) * up).astype(BF16)
    o_ref[...] += jnp.dot(act, wd_ref[...], preferred_element_type=F32)

    if final_norm:
        @pl.when(j == pl.num_programs(1) - 1)
        def _():
            o_ref[...] = _rms(o_ref[...], gf_ref[...])


def _ffn(x, gain, w_gate, w_up, w_down, final_gain, layer, final_norm):
    m, d = x.shape
    f = w_gate.shape[-1]
    return pl.pallas_call(
        functools.partial(_ffn_kernel, final_norm=final_norm),
        grid=(m // FFN_TM, f // FFN_TF),
        in_specs=[pl.BlockSpec((FFN_TM, d), lambda i, j: (i, 0)),
                  pl.BlockSpec((None, 1, d), lambda i, j: (layer, 0, 0)),
                  pl.BlockSpec((None, d, FFN_TF), lambda i, j: (layer, 0, j)),
                  pl.BlockSpec((None, d, FFN_TF), lambda i, j: (layer, 0, j)),
                  pl.BlockSpec((None, FFN_TF, d), lambda i, j: (layer, j, 0)),
                  pl.BlockSpec((1, d), lambda i, j: (0, 0))],
        out_specs=pl.BlockSpec((FFN_TM, d), lambda i, j: (i, 0)),
        out_shape=jax.ShapeDtypeStruct((m, d), F32),
        scratch_shapes=[pltpu.VMEM((FFN_TM, d), BF16)],
        compiler_params=_params(("parallel", "arbitrary")),
        name="ffn",
    )(x, gain, w_gate, w_up, w_down, final_gain)


def _rotary_tables():
    half = RET_QK_DIM // 2
    inv = 1.0 / (ROPE_BASE ** (jnp.arange(half, dtype=F32) / half))
    ang = jnp.arange(SEQ, dtype=F32)[:, None] * inv[None, :]
    return jnp.cos(ang), jnp.sin(ang)


def kernel(x, ln_mix_g, w_in, ret_log_gamma, ssm_a_re, ssm_a_im, ssm_log_dt, ssm_b_re, ssm_b_im,
           ssm_c_re, ssm_c_im, ssm_d, w_glu, b_glu, w_out, ln_ffn_g, w_ffn_gate, w_ffn_up,
           w_ffn_down, ln_final_g):
    depth = w_in.shape[0]
    cos, sin = _rotary_tables()
    row = lambda p: p.astype(F32).reshape(depth, 1, D_MODEL)
    mix_gain, ffn_gain, glu_bias = row(ln_mix_g), row(ln_ffn_g), row(b_glu)
    d_t = jnp.tile(ssm_d.astype(F32).reshape(depth, SSM_GROUPS, 1, SSM_GROUP), (1, 1, 1, SSM_T))
    final_gain = ln_final_g.astype(F32).reshape(1, D_MODEL)
    log_gamma = ret_log_gamma.astype(F32)
    w_in_b, w_glu_b, w_out_b = w_in.astype(BF16), w_glu.astype(BF16), w_out.astype(BF16)
    w_gate_b, w_up_b, w_down_b = (w_ffn_gate.astype(BF16), w_ffn_up.astype(BF16),
                                  w_ffn_down.astype(BF16))
    fac, coeffs = _ssm_factors(ssm_a_re, ssm_a_im, ssm_log_dt, ssm_b_re, ssm_b_im,
                               ssm_c_re, ssm_c_im)
    w_intra, w_state, w_inter_t = _ssm_weights(fac)

    xt = x.reshape(TOKENS, D_MODEL)
    for i in range(depth):
        proj3, u_t = _norm_proj(xt.reshape(BATCH, SEQ, D_MODEL), mix_gain, w_in_b, i)
        ret_g3 = _retention(proj3, log_gamma, cos, sin, i)
        a_lb = _ssm_unpack(_ssm(u_t, w_intra, w_state, w_inter_t, coeffs, d_t, i))
        merged3 = _glu_merge(a_lb, proj3, w_glu_b, glu_bias, ret_g3, i)
        xt = _proj_residual(merged3.reshape(TOKENS, D_MODEL), w_out_b, xt, i)
        xt = _ffn(xt, ffn_gain, w_gate_b, w_up_b, w_down_b, final_gain, i,
                  final_norm=(i == depth - 1))
    return xt.reshape(BATCH, SEQ, D_MODEL)
```
